```python
import jax, jax.numpy as jnp
from jax import lax
import numpy as np

D_MODEL = 4096
BATCH = 4
SEQ = 2048
DEPTH = 1
DEC_BATCH = 32
DEC_SEQ = 4
PAST_LEN = 8192
PAGE_SIZE = 128

FOX_HEADS = 16
FOX_HEAD_DIM = 128
FOX_WIDTH = FOX_HEADS * FOX_HEAD_DIM
RET_HEADS = 8
RET_KEY_DIM = 256
RET_VAL_DIM = 256
RET_K_WIDTH = RET_HEADS * RET_KEY_DIM
RET_V_WIDTH = RET_HEADS * RET_VAL_DIM
D_FF = 4 * D_MODEL
PLE_DIM = 256
Q_BLOCK = 128
RET_CHUNK = 128
ROPE_BASE = 10000.0
NORM_EPS = 1e-6
GN_EPS = 1e-5
FORGET_BIAS_INIT = 3.0
IN_SPLITS = (FOX_WIDTH, FOX_WIDTH, FOX_WIDTH, FOX_HEADS, RET_K_WIDTH, RET_K_WIDTH, RET_V_WIDTH, RET_V_WIDTH)
IN_WIDTH = sum(IN_SPLITS)

kernel_name = 'fox_retention_gated_hybrid_step'


def rmsnorm(x, g):
    xf = x.astype(jnp.float32)
    var = jnp.mean(xf * xf, axis=-1, keepdims=True)
    return (xf * lax.rsqrt(var + NORM_EPS) * g.astype(jnp.float32)).astype(x.dtype)


def rotary(x, pos):
    half = x.shape[-1] // 2
    inv = ROPE_BASE ** (-jnp.arange(half, dtype=jnp.float32) / half)
    ang = pos.astype(jnp.float32)[:, None] * inv[None, :]
    cos = jnp.cos(ang)[None, :, None, :]
    sin = jnp.sin(ang)[None, :, None, :]
    x1 = x[..., :half].astype(jnp.float32)
    x2 = x[..., half:].astype(jnp.float32)
    return jnp.concatenate([x1 * cos - x2 * sin, x2 * cos + x1 * sin], axis=-1)


def ret_log_gammas():
    return jnp.log(1.0 - jnp.exp2(-5.0 - jnp.arange(RET_HEADS, dtype=jnp.float32)))


def project_in(h, w_in, b_f, pos):
    bsz, seq = h.shape[0], h.shape[1]
    offsets = [int(o) for o in np.cumsum(IN_SPLITS)[:-1]]
    z = h @ w_in
    qa, ka, va, fa, qb, kb, vb, gb = jnp.split(z, offsets, axis=-1)
    qa = qa.reshape(bsz, seq, FOX_HEADS, FOX_HEAD_DIM)
    ka = ka.reshape(bsz, seq, FOX_HEADS, FOX_HEAD_DIM)
    va = va.reshape(bsz, seq, FOX_HEADS, FOX_HEAD_DIM)
    logf = jax.nn.log_sigmoid(fa.astype(jnp.float32) + b_f.astype(jnp.float32))
    qb = rotary(qb.reshape(bsz, seq, RET_HEADS, RET_KEY_DIM), pos)
    kb = rotary(kb.reshape(bsz, seq, RET_HEADS, RET_KEY_DIM), pos) * (RET_KEY_DIM ** -0.5)
    vb = vb.reshape(bsz, seq, RET_HEADS, RET_VAL_DIM).astype(jnp.float32)
    return qa, ka, va, logf, qb, kb, vb, gb


def fox_scores(q, k, cq, ck):
    s = jnp.einsum('bqhd,bkhd->bhqk', q, k).astype(jnp.float32) * (FOX_HEAD_DIM ** -0.5)
    return s + jnp.transpose(cq, (0, 2, 1))[:, :, :, None] - jnp.transpose(ck, (0, 2, 1))[:, :, None, :]


def fox_prompt(q, k, v, logf):
    bsz, seq = q.shape[0], q.shape[1]
    c = jnp.cumsum(logf, axis=1)
    kpos = jnp.arange(seq)
    n_blocks = seq // Q_BLOCK

    def one_block(b):
        start = b * Q_BLOCK
        qb = lax.dynamic_slice_in_dim(q, start, Q_BLOCK, axis=1)
        cq = lax.dynamic_slice_in_dim(c, start, Q_BLOCK, axis=1)
        qpos = start + jnp.arange(Q_BLOCK)
        s = fox_scores(qb, k, cq, c)
        s = jnp.where(kpos[None, None, None, :] <= qpos[None, None, :, None], s, -jnp.inf)
        p = jax.nn.softmax(s, axis=-1)
        return jnp.einsum('bhqk,bkhd->bqhd', p.astype(v.dtype), v)

    o = lax.map(one_block, jnp.arange(n_blocks))
    return jnp.transpose(o, (1, 0, 2, 3, 4)).reshape(bsz, seq, FOX_HEADS, FOX_HEAD_DIM)


def fox_sample(q, k_new, v_new, logf_new, k_pages, v_pages, logf_pages):
    dbsz, dseq = q.shape[0], q.shape[1]
    past = k_pages.shape[1] * k_pages.shape[2]
    k_past = k_pages.reshape(dbsz, past, FOX_HEADS, FOX_HEAD_DIM)
    v_past = v_pages.reshape(dbsz, past, FOX_HEADS, FOX_HEAD_DIM)
    c_past = jnp.cumsum(logf_pages.reshape(dbsz, past, FOX_HEADS).astype(jnp.float32), axis=1)
    c_new = c_past[:, -1:, :] + jnp.cumsum(logf_new, axis=1)
    s_past = fox_scores(q, k_past, c_new, c_past)
    s_new = fox_scores(q, k_new, c_new, c_new)
    causal = jnp.arange(dseq)[None, :] <= jnp.arange(dseq)[:, None]
    s_new = jnp.where(causal[None, None], s_new, -jnp.inf)
    p = jax.nn.softmax(jnp.concatenate([s_past, s_new], axis=-1), axis=-1).astype(v_new.dtype)
    return (jnp.einsum('bhqk,bkhd->bqhd', p[..., :past], v_past)
            + jnp.einsum('bhqk,bkhd->bqhd', p[..., past:], v_new))


def retention_chunked(q, k, v, s0, chunk):
    bsz, seq = q.shape[0], q.shape[1]
    n_chunks = seq // chunk
    lg = ret_log_gammas()
    idx = jnp.arange(chunk, dtype=jnp.float32)
    diff = idx[:, None] - idx[None, :]
    dmat = jnp.where(diff[None] >= 0, jnp.exp(lg[:, None, None] * jnp.maximum(diff, 0.0)[None]), 0.0)
    q_dec = jnp.exp(lg[None, :] * (idx[:, None] + 1.0))
    k_dec = jnp.exp(lg[None, :] * (chunk - 1.0 - idx[:, None]))
    chunk_dec = jnp.exp(lg * chunk)

    def to_chunks(t):
        return jnp.transpose(t.reshape(bsz, n_chunks, chunk, RET_HEADS, t.shape[-1]), (1, 0, 2, 3, 4))

    def step(s, inp):
        qc, kc, vc = inp
        inner = jnp.einsum('bihk,bjhk->bhij', qc, kc) * dmat[None]
        o = jnp.einsum('bhij,bjhv->bihv', inner, vc)
        o = o + jnp.einsum('bihk,bhkv->bihv', qc, s) * q_dec[None, :, :, None]
        s_new = s * chunk_dec[None, :, None, None] + jnp.einsum('bjhk,bjhv->bhkv', kc * k_dec[None, :, :, None], vc)
        return s_new, o

    s_fin, o = lax.scan(step, s0.astype(jnp.float32), (to_chunks(q), to_chunks(k), to_chunks(v)))
    o = jnp.transpose(o, (1, 0, 2, 3, 4)).reshape(bsz, seq, RET_HEADS, RET_VAL_DIM)
    return o, s_fin


def finish_layer(x, h, oa, ob, gb, p, w_pa, w_pb, g_ret, w_gate, b_gate, w_o,
                 g_mlp, w_up, w_down, g_ple, w_ple_gate, w_ple):
    bsz, seq = x.shape[0], x.shape[1]
    mu = jnp.mean(ob, axis=-1, keepdims=True)
    var = jnp.mean(jnp.square(ob - mu), axis=-1, keepdims=True)
    ob = (ob - mu) * lax.rsqrt(var + GN_EPS) * g_ret.astype(jnp.float32)
    ob = (jax.nn.silu(gb.astype(jnp.float32)) * ob.reshape(bsz, seq, RET_V_WIDTH)).astype(x.dtype)
    ya = oa.reshape(bsz, seq, FOX_WIDTH) @ w_pa
    yb = ob @ w_pb
    gates = jax.nn.sigmoid((h @ w_gate + b_gate).astype(jnp.float32)).astype(x.dtype)
    ga, gr = gates[..., :D_MODEL], gates[..., D_MODEL:]
    x = x + (ga * ya + gr * yb) @ w_o
    h2 = rmsnorm(x, g_mlp)
    x = x + jnp.square(jax.nn.relu(h2 @ w_up)) @ w_down
    gate = jax.nn.sigmoid((rmsnorm(x, g_ple) @ w_ple_gate).astype(jnp.float32)).astype(x.dtype)
    return x + gate * (p @ w_ple)


def setup_inputs(seed: int = 0) -> dict:
    key = jax.random.key(seed)
    ks = jax.random.split(key, 32)
    f32 = jnp.float32
    n_pages = PAST_LEN // PAGE_SIZE
    n_used = DEC_BATCH * n_pages
    n_phys = n_used + n_used // 4

    def nrm(k, shape, scale=1.0):
        return jax.random.normal(k, shape, f32) * scale

    page_table = jax.random.permutation(ks[6], n_phys)[:n_used].reshape(DEC_BATCH, n_pages).astype(jnp.int32)
    return {
        'x_prompt': nrm(ks[0], (BATCH, SEQ, D_MODEL)),
        'x_sample': nrm(ks[1], (DEC_BATCH, DEC_SEQ, D_MODEL)),
        'cache_k': nrm(ks[2], (DEPTH, n_phys, PAGE_SIZE, FOX_HEADS, FOX_HEAD_DIM)),
        'cache_v': nrm(ks[3], (DEPTH, n_phys, PAGE_SIZE, FOX_HEADS, FOX_HEAD_DIM)),
        'cache_logf': jax.nn.log_sigmoid(FORGET_BIAS_INIT + nrm(ks[4], (DEPTH, n_phys, PAGE_SIZE, FOX_HEADS))),
        'state_ret': nrm(ks[5], (DEPTH, DEC_BATCH, RET_HEADS, RET_KEY_DIM, RET_VAL_DIM)),
        'page_table': page_table,
        'p_prompt': nrm(ks[7], (DEPTH, BATCH, SEQ, PLE_DIM)),
        'p_sample': nrm(ks[8], (DEPTH, DEC_BATCH, DEC_SEQ, PLE_DIM)),
        'g_mix': 1.0 + 0.02 * nrm(ks[9], (DEPTH, D_MODEL)),
        'w_in': nrm(ks[10], (DEPTH, D_MODEL, IN_WIDTH), D_MODEL ** -0.5),
        'b_f': FORGET_BIAS_INIT + 0.5 * nrm(ks[11], (DEPTH, FOX_HEADS)),
        'g_ret': 1.0 + 0.02 * nrm(ks[12], (DEPTH, RET_HEADS, RET_VAL_DIM)),
        'w_pa': nrm(ks[13], (DEPTH, FOX_WIDTH, D_MODEL), FOX_WIDTH ** -0.5),
        'w_pb': nrm(ks[14], (DEPTH, RET_V_WIDTH, D_MODEL), RET_V_WIDTH ** -0.5),
        'w_gate': nrm(ks[15], (DEPTH, D_MODEL, 2 * D_MODEL), D_MODEL ** -0.5),
        'b_gate': 0.02 * nrm(ks[16], (DEPTH, 2 * D_MODEL)),
        'w_o': nrm(ks[17], (DEPTH, D_MODEL, D_MODEL), D_MODEL ** -0.5),
        'g_mlp': 1.0 + 0.02 * nrm(ks[18], (DEPTH, D_MODEL)),
        'w_up': nrm(ks[19], (DEPTH, D_MODEL, D_FF), D_MODEL ** -0.5),
        'w_down': nrm(ks[20], (DEPTH, D_FF, D_MODEL), D_FF ** -0.5),
        'g_ple': 1.0 + 0.02 * nrm(ks[21], (DEPTH, D_MODEL)),
        'w_ple_gate': nrm(ks[22], (DEPTH, D_MODEL, D_MODEL), D_MODEL ** -0.5),
        'w_ple': nrm(ks[23], (DEPTH, PLE_DIM, D_MODEL), PLE_DIM ** -0.5),
        'g_final': 1.0 + 0.02 * nrm(ks[24], (D_MODEL,)),
    }


def reference(x_prompt, x_sample, cache_k, cache_v, cache_logf, state_ret, page_table,
              p_prompt, p_sample, g_mix, w_in, b_f, g_ret, w_pa, w_pb, w_gate, b_gate, w_o,
              g_mlp, w_up, w_down, g_ple, w_ple_gate, w_ple, g_final):
    bsz, seq = x_prompt.shape[0], x_prompt.shape[1]
    dseq = x_sample.shape[1]
    past = page_table.shape[1] * PAGE_SIZE
    pos_prompt = jnp.arange(seq, dtype=jnp.int32)
    pos_sample = past + jnp.arange(dseq, dtype=jnp.int32)
    xp, xs = x_prompt, x_sample
    kp_l, vp_l, fp_l, rp_l = [], [], [], []
    ks_l, vs_l, fs_l, rs_l = [], [], [], []
    for i in range(DEPTH):
        h = rmsnorm(xp, g_mix[i])
        qa, ka, va, logf, qb, kb, vb, gb = project_in(h, w_in[i], b_f[i], pos_prompt)
        oa = fox_prompt(qa, ka, va, logf)
        s0 = jnp.zeros((bsz, RET_HEADS, RET_KEY_DIM, RET_VAL_DIM), jnp.float32)
        ob, sp = retention_chunked(qb, kb, vb, s0, RET_CHUNK)
        xp = finish_layer(xp, h, oa, ob, gb, p_prompt[i], w_pa[i], w_pb[i], g_ret[i], w_gate[i], b_gate[i],
                          w_o[i], g_mlp[i], w_up[i], w_down[i], g_ple[i], w_ple_gate[i], w_ple[i])
        kp_l.append(ka); vp_l.append(va); fp_l.append(logf); rp_l.append(sp)
        h = rmsnorm(xs, g_mix[i])
        qa, ka, va, logf, qb, kb, vb, gb = project_in(h, w_in[i], b_f[i], pos_sample)
        oa = fox_sample(qa, ka, va, logf, cache_k[i, page_table], cache_v[i, page_table],
                        cache_logf[i, page_table])
        ob, ss = retention_chunked(qb, kb, vb, state_ret[i], dseq)
        xs = finish_layer(xs, h, oa, ob, gb, p_sample[i], w_pa[i], w_pb[i], g_ret[i], w_gate[i], b_gate[i],
                          w_o[i], g_mlp[i], w_up[i], w_down[i], g_ple[i], w_ple_gate[i], w_ple[i])
        ks_l.append(ka); vs_l.append(va); fs_l.append(logf); rs_l.append(ss)
    y_prompt = rmsnorm(xp, g_final)
    y_sample = rmsnorm(xs, g_final)
    return (y_prompt, y_sample, jnp.stack(kp_l), jnp.stack(vp_l), jnp.stack(fp_l), jnp.stack(rp_l),
            jnp.stack(ks_l), jnp.stack(vs_l), jnp.stack(fs_l), jnp.stack(rs_l))
```

```python
import functools

import jax
import jax.numpy as jnp
from jax import lax
from jax.experimental import pallas as pl
from jax.experimental.pallas import tpu as pltpu

F32 = jnp.float32
BF16 = jnp.bfloat16

NORM_EPS = 1e-6
GN_EPS = 1e-5
ROPE_BASE = 10000.0

LANES = 128
VMEM_CAP_BYTES = 60 * 1024 * 1024
VMEM_SLACK_BYTES = 6 * 1024 * 1024

NT_DIMS = (((1,), (1,)), ((), ()))
TN_DIMS = (((0,), (0,)), ((), ()))


def _nbytes(shape, dtype):
    n = 1
    for s in shape:
        n *= s
    return n * jnp.dtype(dtype).itemsize


def _params(semantics, est_bytes):
    limit = min(VMEM_CAP_BYTES, max(32 * 1024 * 1024, est_bytes + VMEM_SLACK_BYTES))
    return pltpu.CompilerParams(dimension_semantics=semantics, vmem_limit_bytes=limit)


def _sigmoid(x):
    return 1.0 / (1.0 + jnp.exp(-x))


def _log_sigmoid(x):
    return -(jnp.maximum(-x, 0.0) + jnp.log1p(jnp.exp(-jnp.abs(x))))


def _rmsnorm_body(x_ref, g_ref, o_ref):
    x = x_ref[...]
    var = jnp.mean(x * x, axis=-1, keepdims=True)
    o_ref[...] = (x * lax.rsqrt(var + NORM_EPS) * g_ref[...]).astype(o_ref.dtype)


def _rmsnorm(x, g, out_dtype, tm):
    m, d = x.shape
    est = 2 * (_nbytes((tm, d), x.dtype) + _nbytes((tm, d), out_dtype)) + _nbytes((tm, d), F32)
    return pl.pallas_call(
        _rmsnorm_body,
        grid=(m // tm,),
        in_specs=[pl.BlockSpec((tm, d), lambda i: (i, 0)), pl.BlockSpec((1, d), lambda i: (0, 0))],
        out_specs=pl.BlockSpec((tm, d), lambda i: (i, 0)),
        out_shape=jax.ShapeDtypeStruct((m, d), out_dtype),
        compiler_params=_params(("arbitrary",), est),
        name="rmsnorm",
    )(x, g.reshape(1, d))


def _norm_fgate_body(x_ref, g_ref, wf_ref, bf_ref, h_ref, lf_ref):
    x = x_ref[...]
    var = jnp.mean(x * x, axis=-1, keepdims=True)
    h = x * lax.rsqrt(var + NORM_EPS) * g_ref[...]
    h_ref[...] = h.astype(h_ref.dtype)
    fa = jnp.dot(h, wf_ref[...], preferred_element_type=F32) + bf_ref[...]
    lf_ref[...] = _log_sigmoid(fa)


def _norm_fgate(x, g, wf_pad, bf_pad, tm):
    m, d = x.shape
    est = 2 * (_nbytes((tm, d), F32) + _nbytes((tm, d), BF16) + _nbytes((d, LANES), F32)) + 2 * _nbytes((tm, d), F32)
    return pl.pallas_call(
        _norm_fgate_body,
        grid=(m // tm,),
        in_specs=[
            pl.BlockSpec((tm, d), lambda i: (i, 0)),
            pl.BlockSpec((1, d), lambda i: (0, 0)),
            pl.BlockSpec((d, LANES), lambda i: (0, 0)),
            pl.BlockSpec((1, LANES), lambda i: (0, 0)),
        ],
        out_specs=[pl.BlockSpec((tm, d), lambda i: (i, 0)), pl.BlockSpec((tm, LANES), lambda i: (i, 0))],
        out_shape=[jax.ShapeDtypeStruct((m, d), BF16), jax.ShapeDtypeStruct((m, LANES), F32)],
        compiler_params=_params(("arbitrary",), est),
        name="norm_fgate",
    )(x, g.reshape(1, d), wf_pad, bf_pad)


def _mm_body(*refs, n_dots, n_extra, epi, k_steps):
    a_refs = refs[:n_dots]
    b_refs = refs[n_dots:2 * n_dots]
    ex_refs = refs[2 * n_dots:2 * n_dots + n_extra]
    o_ref = refs[2 * n_dots + n_extra]
    if k_steps == 1:
        accs = [jnp.dot(a[...], b[...], preferred_element_type=F32) for a, b in zip(a_refs, b_refs)]
        o_ref[...] = epi(accs, [e[...] for e in ex_refs]).astype(o_ref.dtype)
        return
    acc_ref = refs[-1]
    k = pl.program_id(2)

    @pl.when(k == 0)
    def _():
        acc_ref[...] = jnp.zeros_like(acc_ref)

    acc_ref[...] += jnp.dot(a_refs[0][...], b_refs[0][...], preferred_element_type=F32)

    @pl.when(k == k_steps - 1)
    def _():
        o_ref[...] = epi([acc_ref[...]], [e[...] for e in ex_refs]).astype(o_ref.dtype)


def _mm(name, a_list, b_list, n, *, epi, out_dtype, tm, tn, b_offs=None, extras=(), tk=None):
    m = a_list[0].shape[0]
    n_dots = len(a_list)
    b_offs = list(b_offs) if b_offs is not None else [0] * n_dots
    assert m % tm == 0 and n % tn == 0 and all(o % tn == 0 for o in b_offs)
    k0 = a_list[0].shape[1]
    k_steps = 1 if tk is None else k0 // tk
    assert k_steps == 1 or (n_dots == 1 and k0 % tk == 0)

    est = 0
    if k_steps == 1:
        grid = (n // tn, m // tm)
        sem = ("arbitrary", "arbitrary")
        a_specs = [pl.BlockSpec((tm, a.shape[1]), lambda j, i: (i, 0)) for a in a_list]
        b_specs = [pl.BlockSpec((b.shape[0], tn), lambda j, i, o=o // tn: (0, j + o)) for b, o in zip(b_list, b_offs)]
        for a, b in zip(a_list, b_list):
            est += 2 * (_nbytes((tm, a.shape[1]), a.dtype) + _nbytes((b.shape[0], tn), b.dtype))
        scratch = []
    else:
        grid = (n // tn, m // tm, k_steps)
        sem = ("arbitrary", "arbitrary", "arbitrary")
        a_specs = [pl.BlockSpec((tm, tk), lambda j, i, k: (i, k))]
        b_specs = [pl.BlockSpec((tk, tn), lambda j, i, k, o=b_offs[0] // tn: (k, j + o))]
        est += 2 * (_nbytes((tm, tk), a_list[0].dtype) + _nbytes((tk, tn), b_list[0].dtype))
        scratch = [pltpu.VMEM((tm, tn), F32)]
        est += _nbytes((tm, tn), F32)
    ex_specs = []
    for arr, off in extras:
        assert off % tn == 0
        rows = 1 if arr.shape[0] == 1 else tm
        if k_steps == 1:
            imap = (lambda j, i, o=off // tn: (0, j + o)) if rows == 1 else (lambda j, i, o=off // tn: (i, j + o))
        else:
            imap = (lambda j, i, k, o=off // tn: (0, j + o)) if rows == 1 else (lambda j, i, k, o=off // tn: (i, j + o))
        ex_specs.append(pl.BlockSpec((rows, tn), imap))
        est += 2 * _nbytes((rows, tn), arr.dtype)
    out_map = (lambda j, i: (i, j)) if k_steps == 1 else (lambda j, i, k: (i, j))
    est += 2 * _nbytes((tm, tn), out_dtype) + (n_dots + 2) * _nbytes((tm, tn), F32)
    body = functools.partial(_mm_body, n_dots=n_dots, n_extra=len(extras), epi=epi, k_steps=k_steps)
    return pl.pallas_call(
        body,
        grid=grid,
        in_specs=a_specs + b_specs + ex_specs,
        out_specs=pl.BlockSpec((tm, tn), out_map),
        out_shape=jax.ShapeDtypeStruct((m, n), out_dtype),
        scratch_shapes=scratch,
        compiler_params=_params(sem, est),
        name=name,
    )(*a_list, *b_list, *[arr for arr, _ in extras])


def _epi_id(accs, ex):
    return accs[0]


def _epi_sigmoid_bias(accs, ex):
    return _sigmoid(accs[0] + ex[0])


def _epi_merge(accs, ex):
    return ex[0] * accs[0] + ex[1] * accs[1]


def _epi_residual(accs, ex):
    return ex[0] + accs[0]


def _epi_relu2(accs, ex):
    return jnp.square(jnp.maximum(accs[0], 0.0))


def _epi_ple(accs, ex):
    return ex[0] + _sigmoid(accs[0]) * accs[1]


def _rope_body(cos_ref, sin_ref, *, base_pos):
    n, half = cos_ref.shape
    pos = (lax.broadcasted_iota(jnp.int32, (n, half), 0) + base_pos).astype(F32)
    idx = lax.broadcasted_iota(jnp.int32, (n, half), 1).astype(F32)
    inv = ROPE_BASE ** (-idx / half)
    ang = pos * inv
    cos_ref[...] = jnp.cos(ang)
    sin_ref[...] = jnp.sin(ang)


def _rope_tables(n_rows, half, base_pos):
    return pl.pallas_call(
        functools.partial(_rope_body, base_pos=base_pos),
        out_shape=[jax.ShapeDtypeStruct((n_rows, half), F32)] * 2,
        name="rope_tables",
    )()


def _rotate(x, cos, sin):
    half = x.shape[-1] // 2
    x1 = x[:, :half]
    x2 = x[:, half:]
    return jnp.concatenate([x1 * cos - x2 * sin, x2 * cos + x1 * sin], axis=1)


def _cumsum_body(lf_ref, ct_ref, *, blk):
    s_len = lf_ref.shape[0]
    n_heads = ct_ref.shape[1]
    r = lax.broadcasted_iota(jnp.int32, (blk, blk), 0)
    c = lax.broadcasted_iota(jnp.int32, (blk, blk), 1)
    upper = (r <= c).astype(F32)
    carry = jnp.zeros((LANES, 1), F32)
    for j in range(s_len // blk):
        x = lf_ref[j * blk:(j + 1) * blk, :]
        loc = lax.dot_general(x, upper, TN_DIMS, precision=lax.Precision.HIGHEST,
                              preferred_element_type=F32) + carry
        ct_ref[0, :, j * blk:(j + 1) * blk] = loc[:n_heads, :]
        carry = loc[:, blk - 1:blk]


def _prompt_cumsum(lf_pad, bsz, seq, n_heads):
    blk = 256
    return pl.pallas_call(
        functools.partial(_cumsum_body, blk=blk),
        grid=(bsz,),
        in_specs=[pl.BlockSpec((seq, LANES), lambda b: (b, 0))],
        out_specs=pl.BlockSpec((1, n_heads, seq), lambda b: (b, 0, 0)),
        out_shape=jax.ShapeDtypeStruct((bsz, n_heads, seq), F32),
        compiler_params=_params(("arbitrary",), 4 * _nbytes((seq, LANES), F32)),
        name="forget_cumsum",
    )(lf_pad)


def _fox_prompt_body(q_ref, k_ref, v_ref, c_ref, o_ref, *, tq, scale):
    i = pl.program_id(2)
    nsub = tq // LANES
    q = q_ref[...]
    cq = jnp.concatenate(
        [jnp.broadcast_to(c_ref[0, 0, pl.ds(i * nsub + a, 1), :], (LANES, LANES)).T for a in range(nsub)], axis=0)
    cq = jnp.concatenate([cq] * nsub, axis=1)

    def scores(j):
        kb = k_ref[pl.ds(pl.multiple_of(j * tq, tq), tq), :]
        s = lax.dot_general(q, kb, NT_DIMS, preferred_element_type=F32) * scale
        ck = jnp.concatenate([c_ref[0, 0, pl.ds(j * nsub + a, 1), :] for a in range(nsub)], axis=1)
        return s + cq - ck

    def update(j, s, carry):
        m, l, acc = carry
        m_new = jnp.maximum(m, jnp.max(s, axis=1, keepdims=True))
        alpha = jnp.exp(m - m_new)
        p = jnp.exp(s - m_new)
        l = alpha * l + jnp.sum(p, axis=1, keepdims=True)
        vb = v_ref[pl.ds(pl.multiple_of(j * tq, tq), tq), :]
        acc = alpha * acc + jnp.dot(p, vb, preferred_element_type=F32)
        return m_new, l, acc

    init = (jnp.full((tq, 1), -jnp.inf, F32), jnp.zeros((tq, 1), F32), jnp.zeros((tq, q.shape[1]), F32))
    carry = lax.fori_loop(0, i, lambda j, c: update(j, scores(j), c), init)
    row = lax.broadcasted_iota(jnp.int32, (tq, tq), 0)
    col = lax.broadcasted_iota(jnp.int32, (tq, tq), 1)
    s_diag = jnp.where(col <= row, scores(i), -jnp.inf)
    _, l, acc = update(i, s_diag, carry)
    o_ref[...] = (acc / l).astype(o_ref.dtype)


def _fox_prompt(q, k, v, ct, bsz, seq, n_heads, head_dim):
    tq = 256
    nq = seq // tq
    c4 = ct.reshape(bsz, n_heads, seq // LANES, LANES)
    est = 4 * _nbytes((seq, head_dim), F32) + 4 * _nbytes((tq, head_dim), F32) + 8 * _nbytes((tq, tq), F32)
    return pl.pallas_call(
        functools.partial(_fox_prompt_body, tq=tq, scale=head_dim ** -0.5),
        grid=(bsz, n_heads, nq),
        in_specs=[
            pl.BlockSpec((tq, head_dim), lambda b, h, i: (b * nq + i, h)),
            pl.BlockSpec((seq, head_dim), lambda b, h, i: (b, h)),
            pl.BlockSpec((seq, head_dim), lambda b, h, i: (b, h)),
            pl.BlockSpec((1, 1, seq // LANES, LANES), lambda b, h, i: (b, h, 0, 0)),
        ],
        out_specs=pl.BlockSpec((tq, head_dim), lambda b, h, i: (b * nq + i, h)),
        out_shape=jax.ShapeDtypeStruct((bsz * seq, n_heads * head_dim), BF16),
        compiler_params=_params(("arbitrary", "arbitrary", "arbitrary"), est),
        name="fox_prompt",
    )(q, k, v, c4)


def _log_gamma(h, shape):
    hv = jnp.full(shape, h, jnp.int32).astype(F32)
    return jnp.log(1.0 - jnp.exp2(-5.0 - hv))


def _retention_chunk(q, k, v, state, lg, n_tok):
    c = q.shape[0]
    ri = lax.broadcasted_iota(jnp.int32, (c, c), 0).astype(F32)
    ci = lax.broadcasted_iota(jnp.int32, (c, c), 1).astype(F32)
    diff = ri - ci
    dmat = jnp.where(diff >= 0, jnp.exp(lg * jnp.maximum(diff, 0.0)), 0.0)
    idx = lax.broadcasted_iota(jnp.int32, (c, 1), 0).astype(F32)
    q_dec = jnp.exp(lg * (idx + 1.0))
    k_dec = jnp.exp(lg * (n_tok - 1.0 - idx))
    chunk_dec = jnp.exp(lg * float(n_tok))
    inner = lax.dot_general(q, k, NT_DIMS, preferred_element_type=F32) * dmat
    o = jnp.dot(inner, v, preferred_element_type=F32)
    o = o + jnp.dot(q, state, preferred_element_type=F32) * q_dec
    new_state = state * chunk_dec + lax.dot_general(k * k_dec, v, TN_DIMS, preferred_element_type=F32)
    return o, new_state


def _group_norm_gate(o, gate, g_row):
    mu = jnp.mean(o, axis=-1, keepdims=True)
    d = o - mu
    var = jnp.mean(d * d, axis=-1, keepdims=True)
    normed = d * lax.rsqrt(var + GN_EPS) * g_row
    return (gate * _sigmoid(gate)) * normed


def _ret_prompt_body(q_ref, k_ref, v_ref, g_ref, cos_ref, sin_ref, gr_ref, o_ref, s_ref, *, kscale):
    h = pl.program_id(1)
    c = pl.program_id(2)

    @pl.when(c == 0)
    def _():
        s_ref[...] = jnp.zeros_like(s_ref)

    cos = cos_ref[...]
    sin = sin_ref[...]
    q = _rotate(q_ref[...], cos, sin)
    k = _rotate(k_ref[...], cos, sin) * kscale
    lg = _log_gamma(h, (1, 1))
    o, new_state = _retention_chunk(q, k, v_ref[...], s_ref[0, 0], lg, q.shape[0])
    s_ref[0, 0] = new_state
    o_ref[...] = _group_norm_gate(o, g_ref[...], gr_ref[0]).astype(o_ref.dtype)


def _ret_prompt(zr, cos, sin, g_ret, bsz, seq, n_heads, dk, dv):
    chunk = 128
    nc = seq // chunk
    assert (2 * n_heads * dk) % dv == 0
    kb = n_heads
    vb = 2 * n_heads * dk // dv
    est = 2 * (2 * _nbytes((chunk, dk), F32) + 2 * _nbytes((chunk, dv), F32) + _nbytes((dk, dv), F32)) \
        + 12 * _nbytes((chunk, dk), F32) + 3 * _nbytes((dk, dv), F32)
    return pl.pallas_call(
        functools.partial(_ret_prompt_body, kscale=dk ** -0.5),
        grid=(bsz, n_heads, nc),
        in_specs=[
            pl.BlockSpec((chunk, dk), lambda b, h, c: (b * nc + c, h)),
            pl.BlockSpec((chunk, dk), lambda b, h, c: (b * nc + c, kb + h)),
            pl.BlockSpec((chunk, dv), lambda b, h, c: (b * nc + c, vb + h)),
            pl.BlockSpec((chunk, dv), lambda b, h, c: (b * nc + c, vb + n_heads + h)),
            pl.BlockSpec((chunk, dk // 2), lambda b, h, c: (c, 0)),
            pl.BlockSpec((chunk, dk // 2), lambda b, h, c: (c, 0)),
            pl.BlockSpec((1, 1, dv), lambda b, h, c: (h, 0, 0)),
        ],
        out_specs=[
            pl.BlockSpec((chunk, dv), lambda b, h, c: (b * nc + c, h)),
            pl.BlockSpec((1, 1, dk, dv), lambda b, h, c: (b, h, 0, 0)),
        ],
        out_shape=[
            jax.ShapeDtypeStruct((bsz * seq, n_heads * dv), BF16),
            jax.ShapeDtypeStruct((bsz, n_heads, dk, dv), F32),
        ],
        compiler_params=_params(("arbitrary", "arbitrary", "arbitrary"), est),
        name="retention_prompt",
    )(zr, zr, zr, zr, cos, sin, g_ret.reshape(n_heads, 1, dv))


def _ret_sample_body(z_ref, st_ref, cos_ref, sin_ref, gr_ref, o_ref, so_ref, *, n_heads, dk, dv, n_tok, kscale):
    cos = cos_ref[:n_tok, :]
    sin = sin_ref[:n_tok, :]
    kw = n_heads * dk
    vw = n_heads * dv
    for h in range(n_heads):
        q = _rotate(z_ref[0, :, h * dk:(h + 1) * dk], cos, sin)
        k = _rotate(z_ref[0, :, kw + h * dk:kw + (h + 1) * dk], cos, sin) * kscale
        v = z_ref[0, :, 2 * kw + h * dv:2 * kw + (h + 1) * dv]
        gate = z_ref[0, :, 2 * kw + vw + h * dv:2 * kw + vw + (h + 1) * dv]
        o, new_state = _retention_chunk(q, k, v, st_ref[0, h], _log_gamma(h, (1, 1)), n_tok)
        so_ref[0, h] = new_state
        o_ref[0, :, h * dv:(h + 1) * dv] = _group_norm_gate(o, gate, gr_ref[h:h + 1, :]).astype(o_ref.dtype)


def _ret_sample(zr3, state, cos, sin, g_ret):
    dbsz, n_tok, width = zr3.shape
    _, n_heads, dk, dv = state.shape
    est = 4 * _nbytes((n_heads, dk, dv), F32) + 4 * _nbytes((8, width), F32) + 4 * _nbytes((dk, dv), F32)
    return pl.pallas_call(
        functools.partial(_ret_sample_body, n_heads=n_heads, dk=dk, dv=dv, n_tok=n_tok, kscale=dk ** -0.5),
        grid=(dbsz,),
        in_specs=[
            pl.BlockSpec((1, n_tok, width), lambda b: (b, 0, 0)),
            pl.BlockSpec((1, n_heads, dk, dv), lambda b: (b, 0, 0, 0)),
            pl.BlockSpec(cos.shape, lambda b: (0, 0)),
            pl.BlockSpec(sin.shape, lambda b: (0, 0)),
            pl.BlockSpec((n_heads, dv), lambda b: (0, 0)),
        ],
        out_specs=[
            pl.BlockSpec((1, n_tok, n_heads * dv), lambda b: (b, 0, 0)),
            pl.BlockSpec((1, n_heads, dk, dv), lambda b: (b, 0, 0, 0)),
        ],
        out_shape=[
            jax.ShapeDtypeStruct((dbsz, n_tok, n_heads * dv), BF16),
            jax.ShapeDtypeStruct(state.shape, F32),
        ],
        compiler_params=_params(("arbitrary",), est),
        name="retention_sample",
    )(zr3, state, cos, sin, g_ret)


def _suffix_body(pt_ref, lf_hbm, lfn_ref, suf_ref, cs_ref, buf, sem, *, n_pages, page):
    b = pl.program_id(0)
    n_heads = suf_ref.shape[1]

    def page_copy(p):
        return pltpu.make_async_copy(lf_hbm.at[pt_ref[b, p]], buf.at[p], sem)

    for p in range(n_pages):
        page_copy(p).start()
    for p in range(n_pages):
        page_copy(p).wait()

    r = lax.broadcasted_iota(jnp.int32, (page, 2 * page), 0)
    c = lax.broadcasted_iota(jnp.int32, (page, 2 * page), 1)
    sel = jnp.where(c < page, (r > c).astype(F32), 1.0)

    def step(t, carry):
        p = n_pages - 1 - t
        both = lax.dot_general(buf[p], sel, TN_DIMS, precision=lax.Precision.HIGHEST, preferred_element_type=F32)
        suf_ref[0, :, pl.ds(pl.multiple_of(p * page, page), page)] = both[:, :page] + carry
        return carry + both[:, page:]

    lax.fori_loop(0, n_pages, step, jnp.zeros((n_heads, page), F32))

    x = lfn_ref[0]
    rows = [x[0:1, :]]
    for t in range(1, x.shape[0]):
        rows.append(rows[-1] + x[t:t + 1, :])
    cs_ref[0] = jnp.concatenate(rows, axis=0)


def _sample_forget_bias(page_table, cache_lf, lf_new):
    dbsz, n_pages = page_table.shape
    _, page, n_heads = cache_lf.shape
    n_tok = lf_new.shape[1]
    grid_spec = pltpu.PrefetchScalarGridSpec(
        num_scalar_prefetch=1,
        grid=(dbsz,),
        in_specs=[
            pl.BlockSpec(memory_space=pl.ANY),
            pl.BlockSpec((1, n_tok, n_heads), lambda b, pt: (b, 0, 0)),
        ],
        out_specs=[
            pl.BlockSpec((1, n_heads, n_pages * page), lambda b, pt: (b, 0, 0)),
            pl.BlockSpec((1, n_tok, n_heads), lambda b, pt: (b, 0, 0)),
        ],
        scratch_shapes=[pltpu.VMEM((n_pages, page, n_heads), F32), pltpu.SemaphoreType.DMA(())],
    )
    est = _nbytes((n_pages, page, LANES), F32) + 2 * _nbytes((n_heads, n_pages * page), F32)
    return pl.pallas_call(
        functools.partial(_suffix_body, n_pages=n_pages, page=page),
        grid_spec=grid_spec,
        out_shape=[
            jax.ShapeDtypeStruct((dbsz, n_heads, n_pages * page), F32),
            jax.ShapeDtypeStruct((dbsz, n_tok, n_heads), F32),
        ],
        compiler_params=_params(("arbitrary",), est),
        name="sample_forget_bias",
    )(page_table, cache_lf, lf_new)


def _fox_sample_body(pt_ref, q_ref, kn_ref, vn_ref, suf_ref, csc_ref, bn_ref, *rest,
                     n_pg, page, n_heads, head_dim, n_tok, scale):
    k_refs = rest[:n_pg]
    v_refs = rest[n_pg:2 * n_pg]
    o_ref = rest[2 * n_pg]
    qbd_ref, m_ref, l_ref, acc_ref = rest[2 * n_pg + 1:]
    j = pl.program_id(1)
    rows = n_tok * n_heads
    width = n_heads * head_dim

    @pl.when(j == 0)
    def _():
        q = q_ref[0]
        hr = lax.broadcasted_iota(jnp.int32, (n_heads, width), 0)
        hc = lax.broadcasted_iota(jnp.int32, (n_heads, width), 1) // head_dim
        for t in range(n_tok):
            qbd_ref[t * n_heads:(t + 1) * n_heads, :] = jnp.where(
                hr == hc, jnp.broadcast_to(q[t:t + 1, :], (n_heads, width)), 0.0)
        m_ref[...] = jnp.full_like(m_ref, -jnp.inf)
        l_ref[...] = jnp.zeros_like(l_ref)
        acc_ref[...] = jnp.zeros_like(acc_ref)

    qbd = qbd_ref[...]
    bias = jnp.concatenate([suf_ref[0]] * n_tok, axis=0) + csc_ref[0][:, :1]
    s = jnp.concatenate(
        [lax.dot_general(qbd, k_refs[i][0], NT_DIMS, preferred_element_type=F32) for i in range(n_pg)], axis=1)
    s = s * scale + bias
    m_old = m_ref[...]
    m_new = jnp.maximum(m_old, jnp.max(s, axis=1, keepdims=True))
    alpha = jnp.exp(m_old - m_new)
    p = jnp.exp(s - m_new)
    l_new = alpha * l_ref[...] + jnp.sum(p, axis=1, keepdims=True)
    pv = jnp.dot(p[:, :page], v_refs[0][0], preferred_element_type=F32)
    for i in range(1, n_pg):
        pv = pv + jnp.dot(p[:, i * page:(i + 1) * page], v_refs[i][0], preferred_element_type=F32)
    acc_new = alpha * acc_ref[...] + pv
    m_ref[...] = m_new
    l_ref[...] = l_new
    acc_ref[...] = acc_new

    @pl.when(j == pl.num_programs(1) - 1)
    def _():
        sn = lax.dot_general(qbd, kn_ref[0], NT_DIMS, preferred_element_type=F32) * scale + bn_ref[0]
        m2 = jnp.maximum(m_new, jnp.max(sn, axis=1, keepdims=True))
        a2 = jnp.exp(m_new - m2)
        p2 = jnp.exp(sn - m2)
        l2 = a2 * l_new + jnp.sum(p2, axis=1, keepdims=True)
        o = (a2 * acc_new + jnp.dot(p2, vn_ref[0], preferred_element_type=F32)) / l2
        hr = lax.broadcasted_iota(jnp.int32, (rows, width), 0) % n_heads
        hc = lax.broadcasted_iota(jnp.int32, (rows, width), 1) // head_dim
        o = jnp.where(hr == hc, o, 0.0)
        out = o[:, :head_dim]
        for h in range(1, n_heads):
            out = out + o[:, h * head_dim:(h + 1) * head_dim]
        o_ref[0] = out.astype(o_ref.dtype)


def _fox_sample(page_table, q3, kn_pad, vn_pad, suf, cs_col, bias_new, cache_k, cache_v, n_heads, head_dim):
    dbsz, n_tok, width = q3.shape
    n_pages = page_table.shape[1]
    page = cache_k.shape[1]
    n_pg = 8
    assert n_pages % n_pg == 0
    rows = n_tok * n_heads
    pad_rows = kn_pad.shape[1]

    def page_spec(i):
        return pl.BlockSpec((1, page, width), lambda b, j, pt, i=i: (pt[b, j * n_pg + i], 0, 0))

    grid_spec = pltpu.PrefetchScalarGridSpec(
        num_scalar_prefetch=1,
        grid=(dbsz, n_pages // n_pg),
        in_specs=[
            pl.BlockSpec((1, n_tok, width), lambda b, j, pt: (b, 0, 0)),
            pl.BlockSpec((1, pad_rows, width), lambda b, j, pt: (b, 0, 0)),
            pl.BlockSpec((1, pad_rows, width), lambda b, j, pt: (b, 0, 0)),
            pl.BlockSpec((1, n_heads, n_pg * page), lambda b, j, pt: (b, 0, j)),
            pl.BlockSpec((1, rows, LANES), lambda b, j, pt: (b, 0, 0)),
            pl.BlockSpec((1, rows, LANES), lambda b, j, pt: (b, 0, 0)),
        ] + [page_spec(i) for i in range(n_pg)] * 2,
        out_specs=pl.BlockSpec((1, rows, head_dim), lambda b, j, pt: (b, 0, 0)),
        scratch_shapes=[
            pltpu.VMEM((rows, width), F32),
            pltpu.VMEM((rows, 1), F32),
            pltpu.VMEM((rows, 1), F32),
            pltpu.VMEM((rows, width), F32),
        ],
    )
    est = 4 * n_pg * _nbytes((page, width), F32) + 4 * _nbytes((pad_rows, width), F32) \
        + 8 * _nbytes((rows, width), F32) + 4 * _nbytes((rows, n_pg * page), F32)
    return pl.pallas_call(
        functools.partial(_fox_sample_body, n_pg=n_pg, page=page, n_heads=n_heads, head_dim=head_dim,
                          n_tok=n_tok, scale=head_dim ** -0.5),
        grid_spec=grid_spec,
        out_shape=jax.ShapeDtypeStruct((dbsz, rows, head_dim), BF16),
        compiler_params=_params(("arbitrary", "arbitrary"), est),
        name="fox_sample",
    )(page_table, q3, kn_pad, vn_pad, suf, cs_col, bias_new, *([cache_k] * n_pg), *([cache_v] * n_pg))


def _mix_inputs(x, w, lw, tm):
    d = x.shape[1]
    fox_w = lw["fox_w"]
    h, lf_pad = _norm_fgate(x, w["g_mix"], lw["wf_pad"], lw["bf_pad"], tm)
    tn = 512
    qkv = [
        _mm("in_proj_fox", [h], [w["w_in"]], fox_w, b_offs=[o * fox_w], epi=_epi_id, out_dtype=F32, tm=tm, tn=tn)
        for o in range(3)
    ]
    zr = _mm("in_proj_ret", [h], [lw["w_ret"]], lw["w_ret"].shape[1], epi=_epi_id, out_dtype=F32, tm=tm, tn=tn)
    gates = _mm("merge_gates", [h], [w["w_gate"]], 2 * d, extras=[(w["b_gate"].reshape(1, 2 * d), 0)],
                epi=_epi_sigmoid_bias, out_dtype=F32, tm=tm, tn=tn)
    return lf_pad, qkv, zr, gates


def _finish(x, oa, ob, gates, p, w, tm, last):
    d = x.shape[1]
    tn = 512
    merged = _mm("branch_merge", [oa, ob], [w["w_pa"], w["w_pb"]], d, extras=[(gates, 0), (gates, d)],
                 epi=_epi_merge, out_dtype=BF16, tm=tm, tn=tn)
    x1 = _mm("out_proj", [merged], [w["w_o"]], d, extras=[(x, 0)], epi=_epi_residual, out_dtype=F32, tm=tm, tn=tn)
    h2 = _rmsnorm(x1, w["g_mlp"], BF16, min(tm, 512))
    up = _mm("mlp_up", [h2], [w["w_up"]], w["w_up"].shape[1], epi=_epi_relu2, out_dtype=BF16, tm=tm, tn=tn)
    x2 = _mm("mlp_down", [up], [w["w_down"]], d, extras=[(x1, 0)], epi=_epi_residual, out_dtype=F32,
             tm=tm, tn=tn, tk=4096 if up.shape[1] > 4096 else None)
    h3 = _rmsnorm(x2, w["g_ple"], BF16, min(tm, 512))
    x3 = _mm("ple", [h3, p], [w["w_ple_gate"], w["w_ple"]], d, extras=[(x2, 0)], epi=_epi_ple, out_dtype=F32,
             tm=tm, tn=tn)
    if last is not None:
        return _rmsnorm(x3, last, F32, min(tm, 512))
    return x3


def kernel(x_prompt, x_sample, cache_k, cache_v, cache_logf, state_ret, page_table, p_prompt, p_sample, g_mix, w_in,
           b_f, g_ret, w_pa, w_pb, w_gate, b_gate, w_o, g_mlp, w_up, w_down, g_ple, w_ple_gate, w_ple, g_final):
    bsz, seq, d = x_prompt.shape
    dbsz, dseq, _ = x_sample.shape
    depth, n_phys, page, ha, da = cache_k.shape
    _, _, hr, dk, dv = state_ret.shape
    past = page_table.shape[1] * page
    fox_w = ha * da
    f_off = 3 * fox_w
    tp, ts = bsz * seq, dbsz * dseq

    cos_p, sin_p = _rope_tables(seq, dk // 2, 0)
    cos_s, sin_s = _rope_tables(8, dk // 2, past)

    xp = x_prompt.reshape(tp, d)
    xs = x_sample.reshape(ts, d)
    outs = [[] for _ in range(8)]
    for i in range(depth):
        w = dict(g_mix=g_mix[i], w_in=w_in[i], w_pa=w_pa[i], w_pb=w_pb[i], w_gate=w_gate[i], b_gate=b_gate[i],
                 w_o=w_o[i], g_mlp=g_mlp[i], w_up=w_up[i], w_down=w_down[i], g_ple=g_ple[i],
                 w_ple_gate=w_ple_gate[i], w_ple=w_ple[i])
        lw = dict(
            fox_w=fox_w,
            wf_pad=jnp.pad(w_in[i][:, f_off:f_off + ha], ((0, 0), (0, LANES - ha))),
            bf_pad=jnp.pad(b_f[i].reshape(1, ha), ((0, 0), (0, LANES - ha))),
            w_ret=w_in[i][:, f_off + ha:],
        )
        last = g_final if i == depth - 1 else None

        lf_pad, (q, k, v), zr, gates = _mix_inputs(xp, w, lw, 1024)
        ct = _prompt_cumsum(lf_pad, bsz, seq, ha)
        oa = _fox_prompt(q, k, v, ct, bsz, seq, ha, da)
        ob, sp = _ret_prompt(zr, cos_p, sin_p, g_ret[i], bsz, seq, hr, dk, dv)
        xp = _finish(xp, oa, ob, gates, p_prompt[i].reshape(tp, -1), w, 1024, last)
        outs[0].append(k.reshape(bsz, seq, ha, da))
        outs[1].append(v.reshape(bsz, seq, ha, da))
        outs[2].append(lf_pad[:, :ha].reshape(bsz, seq, ha))
        outs[3].append(sp)

        lf_pad, (q, k, v), zr, gates = _mix_inputs(xs, w, lw, ts)
        lf_new = lf_pad[:, :ha].reshape(dbsz, dseq, ha)
        suf, cs = _sample_forget_bias(page_table, cache_logf[i], lf_new)
        rows = dseq * ha
        cs_col = jnp.broadcast_to(cs.reshape(dbsz, rows, 1), (dbsz, rows, LANES))
        diff = cs[:, :, None, :] - cs[:, None, :, :]
        causal = (jnp.arange(dseq)[None, :] <= jnp.arange(dseq)[:, None])[None, :, :, None]
        diff = jnp.transpose(jnp.where(causal, diff, -jnp.inf), (0, 1, 3, 2)).reshape(dbsz, rows, dseq)
        bias_new = jnp.pad(diff, ((0, 0), (0, 0), (0, LANES - dseq)), constant_values=-jnp.inf)
        pad_new = lambda t: jnp.pad(t.reshape(dbsz, dseq, fox_w), ((0, 0), (0, LANES - dseq), (0, 0)))
        oa = _fox_sample(page_table, q.reshape(dbsz, dseq, fox_w), pad_new(k), pad_new(v), suf, cs_col, bias_new,
                         cache_k[i].reshape(n_phys, page, fox_w), cache_v[i].reshape(n_phys, page, fox_w), ha, da)
        ob, ss = _ret_sample(zr.reshape(dbsz, dseq, -1), state_ret[i], cos_s, sin_s, g_ret[i])
        xs = _finish(xs, oa.reshape(ts, fox_w), ob.reshape(ts, hr * dv), gates, p_sample[i].reshape(ts, -1), w, ts,
                     last)
        outs[4].append(k.reshape(dbsz, dseq, ha, da))
        outs[5].append(v.reshape(dbsz, dseq, ha, da))
        outs[6].append(lf_new)
        outs[7].append(ss)

    return (xp.reshape(bsz, seq, d), xs.reshape(dbsz, dseq, d), *[jnp.stack(o) for o in outs])
```

```python
import functools

import jax
import jax.numpy as jnp
from jax import lax
from jax.experimental import pallas as pl
from jax.experimental.pallas import tpu as pltpu

F32 = jnp.float32
BF16 = jnp.bfloat16

NORM_EPS = 1e-6
GN_EPS = 1e-5
ROPE_BASE = 10000.0

LANES = 128
VMEM_CAP_BYTES = 60 * 1024 * 1024
VMEM_SLACK_BYTES = 6 * 1024 * 1024

NT_DIMS = (((1,), (1,)), ((), ()))
TN_DIMS = (((0,), (0,)), ((), ()))


def _nbytes(shape, dtype):
    n = 1
    for s in shape:
        n *= s
    return n * jnp.dtype(dtype).itemsize


def _params(semantics, est_bytes):
    limit = min(VMEM_CAP_BYTES, max(32 * 1024 * 1024, est_bytes + VMEM_SLACK_BYTES))
    return pltpu.CompilerParams(dimension_semantics=semantics, vmem_limit_bytes=limit)


def _sigmoid(x):
    return 1.0 / (1.0 + jnp.exp(-x))


def _log_sigmoid(x):
    return -(jnp.maximum(-x, 0.0) + jnp.log1p(jnp.exp(-jnp.abs(x))))


def _rmsnorm_body(x_ref, g_ref, o_ref):
    x = x_ref[...]
    var = jnp.mean(x * x, axis=-1, keepdims=True)
    o_ref[...] = (x * lax.rsqrt(var + NORM_EPS) * g_ref[...]).astype(o_ref.dtype)


def _rmsnorm(x, g, out_dtype, tm):
    m, d = x.shape
    est = 2 * (_nbytes((tm, d), x.dtype) + _nbytes((tm, d), out_dtype)) + _nbytes((tm, d), F32)
    return pl.pallas_call(
        _rmsnorm_body,
        grid=(m // tm,),
        in_specs=[pl.BlockSpec((tm, d), lambda i: (i, 0)), pl.BlockSpec((1, d), lambda i: (0, 0))],
        out_specs=pl.BlockSpec((tm, d), lambda i: (i, 0)),
        out_shape=jax.ShapeDtypeStruct((m, d), out_dtype),
        compiler_params=_params(("arbitrary",), est),
        name="rmsnorm",
    )(x, g.reshape(1, d))


def _norm_fgate_body(x_ref, g_ref, wf_ref, bf_ref, h_ref, lf_ref):
    x = x_ref[...]
    var = jnp.mean(x * x, axis=-1, keepdims=True)
    h = x * lax.rsqrt(var + NORM_EPS) * g_ref[...]
    h_ref[...] = h.astype(h_ref.dtype)
    fa = lax.dot_general(h, wf_ref[...], NT_DIMS, preferred_element_type=F32) + bf_ref[...]
    lf_ref[...] = _log_sigmoid(fa)


def _norm_fgate(x, g, wf_rows, bf_pad, tm):
    m, d = x.shape
    est = 2 * (_nbytes((tm, d), F32) + _nbytes((tm, d), BF16) + _nbytes((LANES, d), F32)) + 2 * _nbytes((tm, d), F32)
    return pl.pallas_call(
        _norm_fgate_body,
        grid=(m // tm,),
        in_specs=[
            pl.BlockSpec((tm, d), lambda i: (i, 0)),
            pl.BlockSpec((1, d), lambda i: (0, 0)),
            pl.BlockSpec((LANES, d), lambda i: (0, 0)),
            pl.BlockSpec((1, LANES), lambda i: (0, 0)),
        ],
        out_specs=[pl.BlockSpec((tm, d), lambda i: (i, 0)), pl.BlockSpec((tm, LANES), lambda i: (i, 0))],
        out_shape=[jax.ShapeDtypeStruct((m, d), BF16), jax.ShapeDtypeStruct((m, LANES), F32)],
        compiler_params=_params(("arbitrary",), est),
        name="norm_fgate",
    )(x, g.reshape(1, d), wf_rows, bf_pad)


def _mm_body(*refs, n_dots, n_extra, epi, k_steps, b_rows):
    a_refs = refs[:n_dots]
    b_refs = refs[n_dots:2 * n_dots]
    ex_refs = refs[2 * n_dots:2 * n_dots + n_extra]
    o_ref = refs[2 * n_dots + n_extra]

    def dot(a, b, rows):
        if rows:
            return lax.dot_general(a, b, NT_DIMS, preferred_element_type=F32)
        return jnp.dot(a, b, preferred_element_type=F32)

    if k_steps == 1:
        accs = [dot(a[...], b[...], r) for a, b, r in zip(a_refs, b_refs, b_rows)]
        o_ref[...] = epi(accs, [e[...] for e in ex_refs]).astype(o_ref.dtype)
        return
    acc_ref = refs[-1]
    k = pl.program_id(2)

    @pl.when(k == 0)
    def _():
        acc_ref[...] = jnp.zeros_like(acc_ref)

    acc_ref[...] += dot(a_refs[0][...], b_refs[0][...], b_rows[0])

    @pl.when(k == k_steps - 1)
    def _():
        o_ref[...] = epi([acc_ref[...]], [e[...] for e in ex_refs]).astype(o_ref.dtype)


def _mm(name, a_list, b_list, n, *, epi, out_dtype, tm, tn, b_offs=None, b_rows=None, extras=(), tk=None):
    m = a_list[0].shape[0]
    n_dots = len(a_list)
    b_offs = list(b_offs) if b_offs is not None else [0] * n_dots
    b_rows = tuple(b_rows) if b_rows is not None else (False,) * n_dots
    assert m % tm == 0 and n % tn == 0 and all(o % tn == 0 for o in b_offs)
    k0 = a_list[0].shape[1]
    k_steps = 1 if tk is None else k0 // tk
    assert k_steps == 1 or (n_dots == 1 and k0 % tk == 0 and not b_rows[0])

    est = 0
    if k_steps == 1:
        grid = (n // tn, m // tm)
        sem = ("arbitrary", "arbitrary")
        a_specs = [pl.BlockSpec((tm, a.shape[1]), lambda j, i: (i, 0)) for a in a_list]
        b_specs = []
        for b, o, rows in zip(b_list, b_offs, b_rows):
            if rows:
                b_specs.append(pl.BlockSpec((tn, b.shape[1]), lambda j, i, o=o // tn: (j + o, 0)))
            else:
                b_specs.append(pl.BlockSpec((b.shape[0], tn), lambda j, i, o=o // tn: (0, j + o)))
        for a in a_list:
            est += 2 * (_nbytes((tm, a.shape[1]), a.dtype) + _nbytes((a.shape[1], tn), F32))
        scratch = []
    else:
        grid = (n // tn, m // tm, k_steps)
        sem = ("arbitrary", "arbitrary", "arbitrary")
        a_specs = [pl.BlockSpec((tm, tk), lambda j, i, k: (i, k))]
        b_specs = [pl.BlockSpec((tk, tn), lambda j, i, k, o=b_offs[0] // tn: (k, j + o))]
        est += 2 * (_nbytes((tm, tk), a_list[0].dtype) + _nbytes((tk, tn), b_list[0].dtype))
        scratch = [pltpu.VMEM((tm, tn), F32)]
        est += _nbytes((tm, tn), F32)
    ex_specs = []
    for arr, off in extras:
        assert off % tn == 0
        rows = 1 if arr.shape[0] == 1 else tm
        if k_steps == 1:
            imap = (lambda j, i, o=off // tn: (0, j + o)) if rows == 1 else (lambda j, i, o=off // tn: (i, j + o))
        else:
            imap = (lambda j, i, k, o=off // tn: (0, j + o)) if rows == 1 else (lambda j, i, k, o=off // tn: (i, j + o))
        ex_specs.append(pl.BlockSpec((rows, tn), imap))
        est += 2 * _nbytes((rows, tn), arr.dtype)
    out_map = (lambda j, i: (i, j)) if k_steps == 1 else (lambda j, i, k: (i, j))
    est += 2 * _nbytes((tm, tn), out_dtype) + (n_dots + 2) * _nbytes((tm, tn), F32)
    body = functools.partial(_mm_body, n_dots=n_dots, n_extra=len(extras), epi=epi, k_steps=k_steps, b_rows=b_rows)
    return pl.pallas_call(
        body,
        grid=grid,
        in_specs=a_specs + b_specs + ex_specs,
        out_specs=pl.BlockSpec((tm, tn), out_map),
        out_shape=jax.ShapeDtypeStruct((m, n), out_dtype),
        scratch_shapes=scratch,
        compiler_params=_params(sem, est),
        name=name,
    )(*a_list, *b_list, *[arr for arr, _ in extras])


def _epi_id(accs, ex):
    return accs[0]


def _epi_sigmoid_bias(accs, ex):
    return _sigmoid(accs[0] + ex[0])


def _epi_merge(accs, ex):
    return ex[0] * accs[0] + ex[1] * accs[1]


def _epi_residual(accs, ex):
    return ex[0] + accs[0]


def _epi_relu2(accs, ex):
    return jnp.square(jnp.maximum(accs[0], 0.0))


def _epi_ple(accs, ex):
    return ex[0] + _sigmoid(accs[0]) * accs[1]


def _rope_body(cos_ref, sin_ref, *, base_pos):
    n, half = cos_ref.shape
    pos = (lax.broadcasted_iota(jnp.int32, (n, half), 0) + base_pos).astype(F32)
    idx = lax.broadcasted_iota(jnp.int32, (n, half), 1).astype(F32)
    inv = ROPE_BASE ** (-idx / half)
    ang = pos * inv
    cos_ref[...] = jnp.cos(ang)
    sin_ref[...] = jnp.sin(ang)


def _rope_tables(n_rows, half, base_pos):
    return pl.pallas_call(
        functools.partial(_rope_body, base_pos=base_pos),
        out_shape=[jax.ShapeDtypeStruct((n_rows, half), F32)] * 2,
        name="rope_tables",
    )()


def _rotate(x, cos, sin):
    half = x.shape[-1] // 2
    x1 = x[:, :half]
    x2 = x[:, half:]
    return jnp.concatenate([x1 * cos - x2 * sin, x2 * cos + x1 * sin], axis=1)


def _cumsum_body(lf_ref, ct_ref, *, blk):
    s_len = lf_ref.shape[0]
    n_heads = ct_ref.shape[1]
    r = lax.broadcasted_iota(jnp.int32, (blk, blk), 0)
    c = lax.broadcasted_iota(jnp.int32, (blk, blk), 1)
    upper = (r <= c).astype(F32)
    carry = jnp.zeros((LANES, 1), F32)
    for j in range(s_len // blk):
        x = lf_ref[j * blk:(j + 1) * blk, :]
        loc = lax.dot_general(x, upper, TN_DIMS, precision=lax.Precision.HIGHEST,
                              preferred_element_type=F32) + carry
        ct_ref[0, :, j * blk:(j + 1) * blk] = loc[:n_heads, :]
        carry = loc[:, blk - 1:blk]


def _prompt_cumsum(lf_pad, bsz, seq, n_heads):
    blk = 256
    return pl.pallas_call(
        functools.partial(_cumsum_body, blk=blk),
        grid=(bsz,),
        in_specs=[pl.BlockSpec((seq, LANES), lambda b: (b, 0))],
        out_specs=pl.BlockSpec((1, n_heads, seq), lambda b: (b, 0, 0)),
        out_shape=jax.ShapeDtypeStruct((bsz, n_heads, seq), F32),
        compiler_params=_params(("arbitrary",), 4 * _nbytes((seq, LANES), F32)),
        name="forget_cumsum",
    )(lf_pad)


def _fox_prompt_body(q_ref, k_ref, v_ref, c_ref, o_ref, *, tq, scale):
    seq = q_ref.shape[0]
    nsub = tq // LANES
    row = lax.broadcasted_iota(jnp.int32, (tq, tq), 0)
    col = lax.broadcasted_iota(jnp.int32, (tq, tq), 1)
    causal = col <= row
    for i in range(seq // tq):
        kv = (i + 1) * tq
        q = q_ref[i * tq:(i + 1) * tq, :]
        cq = jnp.concatenate(
            [jnp.broadcast_to(c_ref[0, 0, i * nsub + a:i * nsub + a + 1, :], (LANES, LANES)).T[:, :1]
             for a in range(nsub)], axis=0)
        ck = jnp.concatenate([c_ref[0, 0, r:r + 1, :] for r in range(kv // LANES)], axis=1)
        s = lax.dot_general(q, k_ref[0:kv, :], NT_DIMS, preferred_element_type=F32) * scale
        s = s + cq - ck
        diag = jnp.where(causal, s[:, i * tq:], -jnp.inf)
        s = diag if i == 0 else jnp.concatenate([s[:, :i * tq], diag], axis=1)
        m = jnp.max(s, axis=1, keepdims=True)
        p = jnp.exp(s - m)
        l = jnp.sum(p, axis=1, keepdims=True)
        o = jnp.dot(p, v_ref[0:kv, :], preferred_element_type=F32)
        o_ref[i * tq:(i + 1) * tq, :] = (o / l).astype(o_ref.dtype)


def _fox_prompt(q, k, v, ct, bsz, seq, n_heads, head_dim):
    tq = 256
    c4 = ct.reshape(bsz, n_heads, seq // LANES, LANES)
    est = 8 * _nbytes((seq, head_dim), F32) + 6 * _nbytes((tq, seq), F32)
    blk = pl.BlockSpec((seq, head_dim), lambda b, h: (b, h))
    return pl.pallas_call(
        functools.partial(_fox_prompt_body, tq=tq, scale=head_dim ** -0.5),
        grid=(bsz, n_heads),
        in_specs=[blk, blk, blk, pl.BlockSpec((1, 1, seq // LANES, LANES), lambda b, h: (b, h, 0, 0))],
        out_specs=blk,
        out_shape=jax.ShapeDtypeStruct((bsz * seq, n_heads * head_dim), BF16),
        compiler_params=_params(("arbitrary", "arbitrary"), est),
        name="fox_prompt",
    )(q, k, v, c4)


def _log_gamma(h, shape):
    hv = jnp.full(shape, h, jnp.int32).astype(F32)
    return jnp.log(1.0 - jnp.exp2(-5.0 - hv))


def _retention_decays(lg, c, n_tok):
    ri = lax.broadcasted_iota(jnp.int32, (c, c), 0).astype(F32)
    ci = lax.broadcasted_iota(jnp.int32, (c, c), 1).astype(F32)
    diff = ri - ci
    dmat = jnp.where(diff >= 0, jnp.exp(lg * jnp.maximum(diff, 0.0)), 0.0)
    idx = lax.broadcasted_iota(jnp.int32, (c, 1), 0).astype(F32)
    q_dec = jnp.exp(lg * (idx + 1.0))
    k_dec = jnp.exp(lg * (n_tok - 1.0 - idx))
    chunk_dec = jnp.exp(lg * float(n_tok))
    return dmat, q_dec, k_dec, chunk_dec


def _retention_chunk(q, k, v, state, decays):
    dmat, q_dec, k_dec, chunk_dec = decays
    inner = lax.dot_general(q, k, NT_DIMS, preferred_element_type=F32) * dmat
    o = jnp.dot(inner, v, preferred_element_type=F32)
    o = o + jnp.dot(q, state, preferred_element_type=F32) * q_dec
    new_state = state * chunk_dec + lax.dot_general(k * k_dec, v, TN_DIMS, preferred_element_type=F32)
    return o, new_state


def _group_norm_gate(o, gate, g_row):
    mu = jnp.mean(o, axis=-1, keepdims=True)
    d = o - mu
    var = jnp.mean(d * d, axis=-1, keepdims=True)
    normed = d * lax.rsqrt(var + GN_EPS) * g_row
    return (gate * _sigmoid(gate)) * normed


def _ret_prompt_body(q_ref, k_ref, v_ref, g_ref, cos_ref, sin_ref, gr_ref, o_ref, s_ref, *, chunk, kscale):
    h = pl.program_id(1)
    seq = q_ref.shape[0]
    decays = _retention_decays(_log_gamma(h, (1, 1)), chunk, chunk)
    state = jnp.zeros(s_ref.shape[2:], F32)
    for c in range(seq // chunk):
        rows = slice(c * chunk, (c + 1) * chunk)
        cos = cos_ref[rows, :]
        sin = sin_ref[rows, :]
        q = _rotate(q_ref[rows, :], cos, sin)
        k = _rotate(k_ref[rows, :], cos, sin) * kscale
        o, state = _retention_chunk(q, k, v_ref[rows, :], state, decays)
        o_ref[rows, :] = _group_norm_gate(o, g_ref[rows, :], gr_ref[0]).astype(o_ref.dtype)
    s_ref[0, 0] = state


def _ret_prompt(zr, cos, sin, g_ret, bsz, seq, n_heads, dk, dv):
    chunk = 256
    assert (2 * n_heads * dk) % dv == 0
    kb = n_heads
    vb = 2 * n_heads * dk // dv
    est = 2 * (2 * _nbytes((seq, dk), F32) + 3 * _nbytes((seq, dv), F32)) + 16 * _nbytes((chunk, dk), F32) \
        + 6 * _nbytes((dk, dv), F32)
    return pl.pallas_call(
        functools.partial(_ret_prompt_body, chunk=chunk, kscale=dk ** -0.5),
        grid=(bsz, n_heads),
        in_specs=[
            pl.BlockSpec((seq, dk), lambda b, h: (b, h)),
            pl.BlockSpec((seq, dk), lambda b, h: (b, kb + h)),
            pl.BlockSpec((seq, dv), lambda b, h: (b, vb + h)),
            pl.BlockSpec((seq, dv), lambda b, h: (b, vb + n_heads + h)),
            pl.BlockSpec((seq, dk // 2), lambda b, h: (0, 0)),
            pl.BlockSpec((seq, dk // 2), lambda b, h: (0, 0)),
            pl.BlockSpec((1, 1, dv), lambda b, h: (h, 0, 0)),
        ],
        out_specs=[
            pl.BlockSpec((seq, dv), lambda b, h: (b, h)),
            pl.BlockSpec((1, 1, dk, dv), lambda b, h: (b, h, 0, 0)),
        ],
        out_shape=[
            jax.ShapeDtypeStruct((bsz * seq, n_heads * dv), BF16),
            jax.ShapeDtypeStruct((bsz, n_heads, dk, dv), F32),
        ],
        compiler_params=_params(("arbitrary", "arbitrary"), est),
        name="retention_prompt",
    )(zr, zr, zr, zr, cos, sin, g_ret.reshape(n_heads, 1, dv))


def _ret_sample_body(z_ref, st_ref, cos_ref, sin_ref, gr_ref, o_ref, so_ref, *, n_heads, dk, dv, n_tok, kscale):
    cos = cos_ref[:n_tok, :]
    sin = sin_ref[:n_tok, :]
    kw = n_heads * dk
    vw = n_heads * dv
    for h in range(n_heads):
        q = _rotate(z_ref[0, :, h * dk:(h + 1) * dk], cos, sin)
        k = _rotate(z_ref[0, :, kw + h * dk:kw + (h + 1) * dk], cos, sin) * kscale
        v = z_ref[0, :, 2 * kw + h * dv:2 * kw + (h + 1) * dv]
        gate = z_ref[0, :, 2 * kw + vw + h * dv:2 * kw + vw + (h + 1) * dv]
        decays = _retention_decays(_log_gamma(h, (1, 1)), n_tok, n_tok)
        o, new_state = _retention_chunk(q, k, v, st_ref[0, h], decays)
        so_ref[0, h] = new_state
        o_ref[0, :, h * dv:(h + 1) * dv] = _group_norm_gate(o, gate, gr_ref[h:h + 1, :]).astype(o_ref.dtype)


def _ret_sample(zr3, state, cos, sin, g_ret):
    dbsz, n_tok, width = zr3.shape
    _, n_heads, dk, dv = state.shape
    est = 4 * _nbytes((n_heads, dk, dv), F32) + 4 * _nbytes((8, width), F32) + 4 * _nbytes((dk, dv), F32)
    return pl.pallas_call(
        functools.partial(_ret_sample_body, n_heads=n_heads, dk=dk, dv=dv, n_tok=n_tok, kscale=dk ** -0.5),
        grid=(dbsz,),
        in_specs=[
            pl.BlockSpec((1, n_tok, width), lambda b: (b, 0, 0)),
            pl.BlockSpec((1, n_heads, dk, dv), lambda b: (b, 0, 0, 0)),
            pl.BlockSpec(cos.shape, lambda b: (0, 0)),
            pl.BlockSpec(sin.shape, lambda b: (0, 0)),
            pl.BlockSpec((n_heads, dv), lambda b: (0, 0)),
        ],
        out_specs=[
            pl.BlockSpec((1, n_tok, n_heads * dv), lambda b: (b, 0, 0)),
            pl.BlockSpec((1, n_heads, dk, dv), lambda b: (b, 0, 0, 0)),
        ],
        out_shape=[
            jax.ShapeDtypeStruct((dbsz, n_tok, n_heads * dv), BF16),
            jax.ShapeDtypeStruct(state.shape, F32),
        ],
        compiler_params=_params(("arbitrary",), est),
        name="retention_sample",
    )(zr3, state, cos, sin, g_ret)


def _suffix_body(pt_ref, lf_hbm, lfn_ref, suf_ref, cs_ref, buf, sem, *, dbsz, n_pages, n_heads, page):
    b = pl.program_id(0)

    def page_copy(seq_idx, p):
        return pltpu.make_async_copy(lf_hbm.at[pt_ref[seq_idx, p]], buf.at[seq_idx * n_pages + p], sem.at[seq_idx])

    @pl.when(b == 0)
    def _():
        def start(r, carry):
            page_copy(r // n_pages, r % n_pages).start()
            return carry

        lax.fori_loop(0, dbsz * n_pages, start, 0)

    def wait(p, carry):
        page_copy(b, p).wait()
        return carry

    lax.fori_loop(0, n_pages, wait, 0)

    r_i = lax.broadcasted_iota(jnp.int32, (page, 2 * page), 0)
    c_i = lax.broadcasted_iota(jnp.int32, (page, 2 * page), 1)
    sel = jnp.where(c_i < page, (r_i > c_i).astype(F32), 1.0)
    rows = n_pages * n_heads
    x = buf[pl.ds(b * n_pages, n_pages)].reshape(rows, page)
    both = jnp.dot(x, sel, precision=lax.Precision.HIGHEST, preferred_element_type=F32)
    total = both[:, page:]
    later = total
    step = n_heads
    while step < rows:
        later = later + jnp.concatenate([later[step:], jnp.zeros((step, page), F32)], axis=0)
        step *= 2
    suf_ref[...] = (both[:, :page] + (later - total)).reshape(n_pages, n_heads, page)

    x_new = lfn_ref[0]
    acc = [x_new[0:1, :]]
    for t in range(1, x_new.shape[0]):
        acc.append(acc[-1] + x_new[t:t + 1, :])
    cs_ref[0] = jnp.concatenate(acc, axis=0)


def _sample_forget_bias(page_table, cache_lf_t, lf_new):
    dbsz, n_pages = page_table.shape
    _, n_heads, page = cache_lf_t.shape
    n_tok = lf_new.shape[1]
    n_rows = dbsz * n_pages
    grid_spec = pltpu.PrefetchScalarGridSpec(
        num_scalar_prefetch=1,
        grid=(dbsz,),
        in_specs=[
            pl.BlockSpec(memory_space=pl.ANY),
            pl.BlockSpec((1, n_tok, n_heads), lambda b, pt: (b, 0, 0)),
        ],
        out_specs=[
            pl.BlockSpec((n_pages, n_heads, page), lambda b, pt: (b, 0, 0)),
            pl.BlockSpec((1, n_tok, n_heads), lambda b, pt: (b, 0, 0)),
        ],
        scratch_shapes=[pltpu.VMEM((n_rows, n_heads, page), F32), pltpu.SemaphoreType.DMA((dbsz,))],
    )
    est = _nbytes((n_rows, n_heads, page), F32) + 12 * _nbytes((n_pages * n_heads, 2 * page), F32)
    return pl.pallas_call(
        functools.partial(_suffix_body, dbsz=dbsz, n_pages=n_pages, n_heads=n_heads, page=page),
        grid_spec=grid_spec,
        out_shape=[
            jax.ShapeDtypeStruct((n_rows, n_heads, page), F32),
            jax.ShapeDtypeStruct((dbsz, n_tok, n_heads), F32),
        ],
        compiler_params=_params(("arbitrary",), est),
        name="sample_forget_bias",
    )(page_table, cache_lf_t, lf_new)


def _fox_sample_body(pt_ref, q_ref, kn_ref, vn_ref, bias_ref, csc_ref, bn_ref, *rest, n_pg, n_heads, scale):
    k_refs = rest[:n_pg]
    v_refs = rest[n_pg:2 * n_pg]
    o_ref = rest[2 * n_pg]
    rowbias_ref, m_ref, l_ref, acc_ref = rest[2 * n_pg + 1:]
    j = pl.program_id(1)
    rows, cols = rowbias_ref.shape

    @pl.when(j == 0)
    def _():
        rh = lax.broadcasted_iota(jnp.int32, (rows, cols), 0) % n_heads
        ch = lax.broadcasted_iota(jnp.int32, (rows, cols), 1) % n_heads
        rowbias_ref[...] = jnp.where(rh == ch, csc_ref[0][:, :1], -jnp.inf)
        m_ref[...] = jnp.full_like(m_ref, -jnp.inf)
        l_ref[...] = jnp.zeros_like(l_ref)
        acc_ref[...] = jnp.zeros_like(acc_ref)

    q = q_ref[0]
    rowbias = rowbias_ref[...]
    s_pages = []
    for i in range(n_pg):
        s = lax.dot_general(q, k_refs[i][0], NT_DIMS, preferred_element_type=F32) * scale
        s_pages.append(s + (rowbias + bias_ref[0, i:i + 1, :]))
    m_old = m_ref[...]
    m_new = m_old
    for s in s_pages:
        m_new = jnp.maximum(m_new, jnp.max(s, axis=1, keepdims=True))
    alpha = jnp.exp(m_old - m_new)
    l_new = alpha * l_ref[...]
    acc_new = alpha * acc_ref[...]
    for i in range(n_pg):
        p = jnp.exp(s_pages[i] - m_new)
        l_new = l_new + jnp.sum(p, axis=1, keepdims=True)
        acc_new = acc_new + jnp.dot(p, v_refs[i][0], preferred_element_type=F32)
    m_ref[...] = m_new
    l_ref[...] = l_new
    acc_ref[...] = acc_new

    @pl.when(j == pl.num_programs(1) - 1)
    def _():
        sn = lax.dot_general(q, kn_ref[0], NT_DIMS, preferred_element_type=F32) * scale + bn_ref[0]
        m2 = jnp.maximum(m_new, jnp.max(sn, axis=1, keepdims=True))
        a2 = jnp.exp(m_new - m2)
        p2 = jnp.exp(sn - m2)
        l2 = a2 * l_new + jnp.sum(p2, axis=1, keepdims=True)
        o = (a2 * acc_new + jnp.dot(p2, vn_ref[0], preferred_element_type=F32)) / l2
        o_ref[0] = o.astype(o_ref.dtype)


def _fox_sample(page_table, q3, kn_pad, vn_pad, bias_rows, cs_col, bias_new, cache_k, cache_v, n_heads):
    dbsz, rows, head_dim = q3.shape
    n_pages = page_table.shape[1]
    cols = cache_k.shape[1]
    n_pg = 8
    assert n_pages % n_pg == 0
    pad_rows = kn_pad.shape[1]

    def page_spec(i):
        return pl.BlockSpec((1, cols, head_dim), lambda b, j, pt, i=i: (pt[b, j * n_pg + i], 0, 0))

    grid_spec = pltpu.PrefetchScalarGridSpec(
        num_scalar_prefetch=1,
        grid=(dbsz, n_pages // n_pg),
        in_specs=[
            pl.BlockSpec((1, rows, head_dim), lambda b, j, pt: (b, 0, 0)),
            pl.BlockSpec((1, pad_rows, head_dim), lambda b, j, pt: (b, 0, 0)),
            pl.BlockSpec((1, pad_rows, head_dim), lambda b, j, pt: (b, 0, 0)),
            pl.BlockSpec((1, n_pg, cols), lambda b, j, pt: (b, j, 0)),
            pl.BlockSpec((1, rows, LANES), lambda b, j, pt: (b, 0, 0)),
            pl.BlockSpec((1, rows, LANES), lambda b, j, pt: (b, 0, 0)),
        ] + [page_spec(i) for i in range(n_pg)] * 2,
        out_specs=pl.BlockSpec((1, rows, head_dim), lambda b, j, pt: (b, 0, 0)),
        scratch_shapes=[
            pltpu.VMEM((rows, cols), F32),
            pltpu.VMEM((rows, 1), F32),
            pltpu.VMEM((rows, 1), F32),
            pltpu.VMEM((rows, head_dim), F32),
        ],
    )
    est = 4 * n_pg * _nbytes((cols, head_dim), F32) + (3 * n_pg + 4) * _nbytes((rows, cols), F32)
    return pl.pallas_call(
        functools.partial(_fox_sample_body, n_pg=n_pg, n_heads=n_heads, scale=head_dim ** -0.5),
        grid_spec=grid_spec,
        out_shape=jax.ShapeDtypeStruct((dbsz, rows, head_dim), BF16),
        compiler_params=_params(("arbitrary", "arbitrary"), est),
        name="fox_sample",
    )(page_table, q3, kn_pad, vn_pad, bias_rows, cs_col, bias_new, *([cache_k] * n_pg), *([cache_v] * n_pg))


def _mix_inputs(x, w, lw, tm):
    d = x.shape[1]
    fox_w = lw["fox_w"]
    h, lf_pad = _norm_fgate(x, w["g_mix"], lw["wf_rows"], lw["bf_pad"], tm)
    tn = 512
    qkv = [
        _mm("in_proj_fox", [h], [lw["w_in_t"]], fox_w, b_offs=[o * fox_w], b_rows=[True], epi=_epi_id,
            out_dtype=F32, tm=tm, tn=tn)
        for o in range(3)
    ]
    zr = _mm("in_proj_ret", [h], [lw["w_ret_t"]], lw["w_ret_t"].shape[0], b_rows=[True], epi=_epi_id,
             out_dtype=F32, tm=tm, tn=tn)
    gates = _mm("merge_gates", [h], [w["w_gate"]], 2 * d, extras=[(w["b_gate"].reshape(1, 2 * d), 0)],
                epi=_epi_sigmoid_bias, out_dtype=F32, tm=tm, tn=tn)
    return lf_pad, qkv, zr, gates


def _finish(x, oa, ob, gates, p, w, tm, last):
    d = x.shape[1]
    tn = 512
    merged = _mm("branch_merge", [oa, ob], [w["w_pa"], w["w_pb"]], d, extras=[(gates, 0), (gates, d)],
                 epi=_epi_merge, out_dtype=BF16, tm=tm, tn=tn)
    x1 = _mm("out_proj", [merged], [w["w_o"]], d, extras=[(x, 0)], epi=_epi_residual, out_dtype=F32, tm=tm, tn=tn)
    h2 = _rmsnorm(x1, w["g_mlp"], BF16, min(tm, 512))
    up = _mm("mlp_up", [h2], [w["w_up"]], w["w_up"].shape[1], epi=_epi_relu2, out_dtype=BF16, tm=tm, tn=tn)
    x2 = _mm("mlp_down", [up], [w["w_down"]], d, extras=[(x1, 0)], epi=_epi_residual, out_dtype=F32,
             tm=tm, tn=tn, tk=4096 if up.shape[1] > 4096 else None)
    h3 = _rmsnorm(x2, w["g_ple"], BF16, min(tm, 512))
    x3 = _mm("ple", [h3, p], [w["w_ple_gate"], w["w_ple"]], d, extras=[(x2, 0)], epi=_epi_ple, out_dtype=F32,
             tm=tm, tn=tn)
    if last is not None:
        return _rmsnorm(x3, last, F32, min(tm, 512))
    return x3


def kernel(x_prompt, x_sample, cache_k, cache_v, cache_logf, state_ret, page_table, p_prompt, p_sample, g_mix, w_in,
           b_f, g_ret, w_pa, w_pb, w_gate, b_gate, w_o, g_mlp, w_up, w_down, g_ple, w_ple_gate, w_ple, g_final):
    bsz, seq, d = x_prompt.shape
    dbsz, dseq, _ = x_sample.shape
    depth, n_phys, page, ha, da = cache_k.shape
    _, _, hr, dk, dv = state_ret.shape
    n_pages = page_table.shape[1]
    past = n_pages * page
    fox_w = ha * da
    f_off = 3 * fox_w
    tp, ts = bsz * seq, dbsz * dseq
    rows = dseq * ha

    cos_p, sin_p = _rope_tables(seq, dk // 2, 0)
    cos_s, sin_s = _rope_tables(8, dk // 2, past)

    xp = x_prompt.reshape(tp, d)
    xs = x_sample.reshape(ts, d)
    outs = [[] for _ in range(8)]
    for i in range(depth):
        w = dict(g_mix=g_mix[i], w_pa=w_pa[i], w_pb=w_pb[i], w_gate=w_gate[i], b_gate=b_gate[i],
                 w_o=w_o[i], g_mlp=g_mlp[i], w_up=w_up[i], w_down=w_down[i], g_ple=g_ple[i],
                 w_ple_gate=w_ple_gate[i], w_ple=w_ple[i])
        w_in_t = jnp.swapaxes(w_in[i], 0, 1)
        lw = dict(
            fox_w=fox_w,
            w_in_t=w_in_t,
            wf_rows=jnp.pad(w_in_t[f_off:f_off + ha], ((0, LANES - ha), (0, 0))),
            bf_pad=jnp.pad(b_f[i].reshape(1, ha), ((0, 0), (0, LANES - ha))),
            w_ret_t=w_in_t[f_off + ha:],
        )
        last = g_final if i == depth - 1 else None

        lf_pad, (q, k, v), zr, gates = _mix_inputs(xp, w, lw, 1024)
        ct = _prompt_cumsum(lf_pad, bsz, seq, ha)
        oa = _fox_prompt(q, k, v, ct, bsz, seq, ha, da)
        ob, sp = _ret_prompt(zr, cos_p, sin_p, g_ret[i], bsz, seq, hr, dk, dv)
        xp = _finish(xp, oa, ob, gates, p_prompt[i].reshape(tp, -1), w, 1024, last)
        outs[0].append(k.reshape(bsz, seq, ha, da))
        outs[1].append(v.reshape(bsz, seq, ha, da))
        outs[2].append(lf_pad[:, :ha].reshape(bsz, seq, ha))
        outs[3].append(sp)

        lf_pad, (q, k, v), zr, gates = _mix_inputs(xs, w, lw, ts)
        lf_new = lf_pad[:, :ha].reshape(dbsz, dseq, ha)
        suf, cs = _sample_forget_bias(page_table, jnp.swapaxes(cache_logf[i], 1, 2), lf_new)
        bias_rows = jnp.swapaxes(suf.reshape(dbsz, n_pages, ha, page), 2, 3).reshape(dbsz, n_pages, page * ha)
        cs_col = jnp.broadcast_to(cs.reshape(dbsz, rows, 1), (dbsz, rows, LANES))
        diff = cs[:, :, :, None, None] - cs[:, None, None, :, :]
        tt = jnp.arange(dseq)
        hh = jnp.arange(ha)
        keep = (tt[None, None, :, None] <= tt[:, None, None, None]) & (hh[None, :, None, None] == hh[None, None, None, :])
        bias_new = jnp.where(keep[None], diff, -jnp.inf).reshape(dbsz, rows, rows)
        bias_new = jnp.pad(bias_new, ((0, 0), (0, 0), (0, LANES - rows)), constant_values=-jnp.inf)
        pad_new = lambda t: jnp.pad(t.reshape(dbsz, rows, da), ((0, 0), (0, LANES - rows), (0, 0)))
        oa = _fox_sample(page_table, q.reshape(dbsz, rows, da), pad_new(k), pad_new(v), bias_rows, cs_col, bias_new,
                         cache_k[i].reshape(n_phys, page * ha, da), cache_v[i].reshape(n_phys, page * ha, da), ha)
        ob, ss = _ret_sample(zr.reshape(dbsz, dseq, -1), state_ret[i], cos_s, sin_s, g_ret[i])
        xs = _finish(xs, oa.reshape(ts, fox_w), ob.reshape(ts, hr * dv), gates, p_sample[i].reshape(ts, -1), w, ts,
                     last)
        outs[4].append(k.reshape(dbsz, dseq, ha, da))
        outs[5].append(v.reshape(dbsz, dseq, ha, da))
        outs[6].append(lf_new)
        outs[7].append(ss)

    return (xp.reshape(bsz, seq, d), xs.reshape(dbsz, dseq, d), *[jnp.stack(o) for o in outs])
```

```python
import functools

import jax
import jax.numpy as jnp
from jax import lax
from jax.experimental import pallas as pl
from jax.experimental.pallas import tpu as pltpu

F32 = jnp.float32
BF16 = jnp.bfloat16

NORM_EPS = 1e-6
GN_EPS = 1e-5
ROPE_BASE = 10000.0

LANES = 128
VMEM_CAP_BYTES = 60 * 1024 * 1024
VMEM_SLACK_BYTES = 6 * 1024 * 1024

NT_DIMS = (((1,), (1,)), ((), ()))
TN_DIMS = (((0,), (0,)), ((), ()))


def _nbytes(shape, dtype):
    n = 1
    for s in shape:
        n *= s
    return n * jnp.dtype(dtype).itemsize


def _params(semantics, est_bytes):
    limit = min(VMEM_CAP_BYTES, max(32 * 1024 * 1024, est_bytes + VMEM_SLACK_BYTES))
    return pltpu.CompilerParams(dimension_semantics=semantics, vmem_limit_bytes=limit)


def _sigmoid(x):
    return 1.0 / (1.0 + jnp.exp(-x))


def _log_sigmoid(x):
    return -(jnp.maximum(-x, 0.0) + jnp.log1p(jnp.exp(-jnp.abs(x))))


def _rmsnorm_body(x_ref, g_ref, o_ref):
    x = x_ref[...]
    var = jnp.mean(x * x, axis=-1, keepdims=True)
    o_ref[...] = (x * lax.rsqrt(var + NORM_EPS) * g_ref[...]).astype(o_ref.dtype)


def _rmsnorm(x, g, out_dtype, tm):
    m, d = x.shape
    est = 2 * (_nbytes((tm, d), x.dtype) + _nbytes((tm, d), out_dtype)) + _nbytes((tm, d), F32)
    return pl.pallas_call(
        _rmsnorm_body,
        grid=(m // tm,),
        in_specs=[pl.BlockSpec((tm, d), lambda i: (i, 0)), pl.BlockSpec((1, d), lambda i: (0, 0))],
        out_specs=pl.BlockSpec((tm, d), lambda i: (i, 0)),
        out_shape=jax.ShapeDtypeStruct((m, d), out_dtype),
        compiler_params=_params(("arbitrary",), est),
        name="rmsnorm",
    )(x, g.reshape(1, d))


def _norm_fgate_body(x_ref, g_ref, wf_ref, bf_ref, h_ref, lf_ref):
    x = x_ref[...]
    var = jnp.mean(x * x, axis=-1, keepdims=True)
    h = x * lax.rsqrt(var + NORM_EPS) * g_ref[...]
    h_ref[...] = h.astype(h_ref.dtype)
    fa = lax.dot_general(h, wf_ref[...], NT_DIMS, preferred_element_type=F32) + bf_ref[...]
    lf_ref[...] = _log_sigmoid(fa)


def _norm_fgate(x, g, wf_rows, bf_pad, tm):
    m, d = x.shape
    est = 2 * (_nbytes((tm, d), F32) + _nbytes((tm, d), BF16) + _nbytes((LANES, d), F32)) + 2 * _nbytes((tm, d), F32)
    return pl.pallas_call(
        _norm_fgate_body,
        grid=(m // tm,),
        in_specs=[
            pl.BlockSpec((tm, d), lambda i: (i, 0)),
            pl.BlockSpec((1, d), lambda i: (0, 0)),
            pl.BlockSpec((LANES, d), lambda i: (0, 0)),
            pl.BlockSpec((1, LANES), lambda i: (0, 0)),
        ],
        out_specs=[pl.BlockSpec((tm, d), lambda i: (i, 0)), pl.BlockSpec((tm, LANES), lambda i: (i, 0))],
        out_shape=[jax.ShapeDtypeStruct((m, d), BF16), jax.ShapeDtypeStruct((m, LANES), F32)],
        compiler_params=_params(("arbitrary",), est),
        name="norm_fgate",
    )(x, g.reshape(1, d), wf_rows, bf_pad)


def _mm_body(*refs, n_dots, n_extra, epi, k_steps, b_rows):
    a_refs = refs[:n_dots]
    b_refs = refs[n_dots:2 * n_dots]
    ex_refs = refs[2 * n_dots:2 * n_dots + n_extra]
    o_ref = refs[2 * n_dots + n_extra]

    def dot(a, b, rows):
        if rows:
            return lax.dot_general(a, b, NT_DIMS, preferred_element_type=F32)
        return jnp.dot(a, b, preferred_element_type=F32)

    if k_steps == 1:
        accs = [dot(a[...], b[...], r) for a, b, r in zip(a_refs, b_refs, b_rows)]
        o_ref[...] = epi(accs, [e[...] for e in ex_refs]).astype(o_ref.dtype)
        return
    k = pl.program_id(2)
    part = dot(a_refs[0][...], b_refs[0][...], b_rows[0])

    @pl.when(k == 0)
    def _():
        o_ref[...] = ex_refs[0][...] + part

    @pl.when(k > 0)
    def _():
        o_ref[...] += part


def _mm(name, a_list, b_list, n, *, epi, out_dtype, tm, tn, b_offs=None, b_rows=None, extras=(), tk=None):
    m = a_list[0].shape[0]
    n_dots = len(a_list)
    b_offs = list(b_offs) if b_offs is not None else [0] * n_dots
    b_rows = tuple(b_rows) if b_rows is not None else (False,) * n_dots
    assert m % tm == 0 and n % tn == 0
    assert all(o % tn == 0 or (rows and o % 8 == 0) for o, rows in zip(b_offs, b_rows))
    k0 = a_list[0].shape[1]
    k_steps = 1 if tk is None else k0 // tk
    if k_steps > 1:
        assert n_dots == 1 and k0 % tk == 0 and not b_rows[0] and epi is _epi_residual and out_dtype == F32

    est = 0
    if k_steps == 1:
        grid = (n // tn, m // tm)
        sem = ("arbitrary", "arbitrary")
        a_specs = [pl.BlockSpec((tm, a.shape[1]), lambda j, i: (i, 0)) for a in a_list]
        b_specs = []
        for b, o, rows in zip(b_list, b_offs, b_rows):
            if rows and o % tn:
                b_specs.append(pl.BlockSpec((pl.Element(tn), pl.Element(b.shape[1])),
                                            lambda j, i, o=o: (pl.multiple_of(o + j * tn, 8), 0)))
            elif rows:
                b_specs.append(pl.BlockSpec((tn, b.shape[1]), lambda j, i, o=o // tn: (j + o, 0)))
            else:
                b_specs.append(pl.BlockSpec((b.shape[0], tn), lambda j, i, o=o // tn: (0, j + o)))
        for a in a_list:
            est += 2 * (_nbytes((tm, a.shape[1]), a.dtype) + _nbytes((a.shape[1], tn), F32))
        scratch = []
    else:
        grid = (n // tn, m // tm, k_steps)
        sem = ("arbitrary", "arbitrary", "arbitrary")
        a_specs = [pl.BlockSpec((tm, tk), lambda j, i, k: (i, k))]
        b_specs = [pl.BlockSpec((tk, tn), lambda j, i, k, o=b_offs[0] // tn: (k, j + o))]
        est += 2 * (_nbytes((tm, tk), a_list[0].dtype) + _nbytes((tk, tn), b_list[0].dtype))
        scratch = []
    ex_specs = []
    for arr, off in extras:
        assert off % tn == 0
        rows = 1 if arr.shape[0] == 1 else tm
        if k_steps == 1:
            imap = (lambda j, i, o=off // tn: (0, j + o)) if rows == 1 else (lambda j, i, o=off // tn: (i, j + o))
        else:
            imap = (lambda j, i, k, o=off // tn: (0, j + o)) if rows == 1 else (lambda j, i, k, o=off // tn: (i, j + o))
        ex_specs.append(pl.BlockSpec((rows, tn), imap))
        est += 2 * _nbytes((rows, tn), arr.dtype)
    out_map = (lambda j, i: (i, j)) if k_steps == 1 else (lambda j, i, k: (i, j))
    est += 2 * _nbytes((tm, tn), out_dtype) + (n_dots + 2) * _nbytes((tm, tn), F32)
    body = functools.partial(_mm_body, n_dots=n_dots, n_extra=len(extras), epi=epi, k_steps=k_steps, b_rows=b_rows)
    return pl.pallas_call(
        body,
        grid=grid,
        in_specs=a_specs + b_specs + ex_specs,
        out_specs=pl.BlockSpec((tm, tn), out_map),
        out_shape=jax.ShapeDtypeStruct((m, n), out_dtype),
        scratch_shapes=scratch,
        compiler_params=_params(sem, est),
        name=name,
    )(*a_list, *b_list, *[arr for arr, _ in extras])


def _epi_id(accs, ex):
    return accs[0]


def _epi_sigmoid_bias(accs, ex):
    return _sigmoid(accs[0] + ex[0])


def _epi_merge(accs, ex):
    return ex[0] * accs[0] + ex[1] * accs[1]


def _epi_residual(accs, ex):
    return ex[0] + accs[0]


def _epi_relu2(accs, ex):
    return jnp.square(jnp.maximum(accs[0], 0.0))


def _epi_ple(accs, ex):
    return ex[0] + _sigmoid(accs[0]) * accs[1]


def _rope_body(cos_ref, sin_ref, *, base_pos):
    n, half = cos_ref.shape
    pos = (lax.broadcasted_iota(jnp.int32, (n, half), 0) + base_pos).astype(F32)
    idx = lax.broadcasted_iota(jnp.int32, (n, half), 1).astype(F32)
    inv = ROPE_BASE ** (-idx / half)
    ang = pos * inv
    cos_ref[...] = jnp.cos(ang)
    sin_ref[...] = jnp.sin(ang)


def _rope_tables(n_rows, half, base_pos):
    return pl.pallas_call(
        functools.partial(_rope_body, base_pos=base_pos),
        out_shape=[jax.ShapeDtypeStruct((n_rows, half), F32)] * 2,
        name="rope_tables",
    )()


def _rotate(x, cos, sin):
    half = x.shape[-1] // 2
    x1 = x[:, :half]
    x2 = x[:, half:]
    return jnp.concatenate([x1 * cos - x2 * sin, x2 * cos + x1 * sin], axis=1)


def _cumsum_body(lf_ref, ct_ref, *, blk):
    s_len = lf_ref.shape[0]
    n_heads = ct_ref.shape[1]
    r = lax.broadcasted_iota(jnp.int32, (blk, blk), 0)
    c = lax.broadcasted_iota(jnp.int32, (blk, blk), 1)
    upper = (r <= c).astype(F32)
    carry = jnp.zeros((LANES, 1), F32)
    for j in range(s_len // blk):
        x = lf_ref[j * blk:(j + 1) * blk, :]
        loc = lax.dot_general(x, upper, TN_DIMS, precision=lax.Precision.HIGHEST,
                              preferred_element_type=F32) + carry
        ct_ref[0, :, j * blk:(j + 1) * blk] = loc[:n_heads, :]
        carry = loc[:, blk - 1:blk]


def _prompt_cumsum(lf_pad, bsz, seq, n_heads):
    blk = 256
    return pl.pallas_call(
        functools.partial(_cumsum_body, blk=blk),
        grid=(bsz,),
        in_specs=[pl.BlockSpec((seq, LANES), lambda b: (b, 0))],
        out_specs=pl.BlockSpec((1, n_heads, seq), lambda b: (b, 0, 0)),
        out_shape=jax.ShapeDtypeStruct((bsz, n_heads, seq), F32),
        compiler_params=_params(("arbitrary",), 4 * _nbytes((seq, LANES), F32)),
        name="forget_cumsum",
    )(lf_pad)


def _fox_prompt_body(q_ref, k_ref, v_ref, c_ref, o_ref, *, tq, scale):
    seq = q_ref.shape[0]
    nsub = tq // LANES
    row = lax.broadcasted_iota(jnp.int32, (tq, tq), 0)
    col = lax.broadcasted_iota(jnp.int32, (tq, tq), 1)
    causal = col <= row
    for i in range(seq // tq):
        kv = (i + 1) * tq
        q = q_ref[i * tq:(i + 1) * tq, :]
        cq = jnp.concatenate(
            [jnp.broadcast_to(c_ref[0, 0, i * nsub + a:i * nsub + a + 1, :], (LANES, LANES)).T[:, :1]
             for a in range(nsub)], axis=0)
        ck = jnp.concatenate([c_ref[0, 0, r:r + 1, :] for r in range(kv // LANES)], axis=1)
        s = lax.dot_general(q, k_ref[0:kv, :], NT_DIMS, preferred_element_type=F32) * scale
        s = s + cq - ck
        diag = jnp.where(causal, s[:, i * tq:], -jnp.inf)
        s = diag if i == 0 else jnp.concatenate([s[:, :i * tq], diag], axis=1)
        m = jnp.max(s, axis=1, keepdims=True)
        p = jnp.exp(s - m)
        l = jnp.sum(p, axis=1, keepdims=True)
        o = jnp.dot(p, v_ref[0:kv, :], preferred_element_type=F32)
        o_ref[i * tq:(i + 1) * tq, :] = (o / l).astype(o_ref.dtype)


def _fox_prompt(q, k, v, ct, bsz, seq, n_heads, head_dim):
    tq = 256
    c4 = ct.reshape(bsz, n_heads, seq // LANES, LANES)
    est = 8 * _nbytes((seq, head_dim), F32) + 6 * _nbytes((tq, seq), F32)
    blk = pl.BlockSpec((seq, head_dim), lambda b, h: (b, h))
    return pl.pallas_call(
        functools.partial(_fox_prompt_body, tq=tq, scale=head_dim ** -0.5),
        grid=(bsz, n_heads),
        in_specs=[blk, blk, blk, pl.BlockSpec((1, 1, seq // LANES, LANES), lambda b, h: (b, h, 0, 0))],
        out_specs=blk,
        out_shape=jax.ShapeDtypeStruct((bsz * seq, n_heads * head_dim), BF16),
        compiler_params=_params(("arbitrary", "arbitrary"), est),
        name="fox_prompt",
    )(q, k, v, c4)


def _log_gamma(h, shape):
    hv = jnp.full(shape, h, jnp.int32).astype(F32)
    return jnp.log(1.0 - jnp.exp2(-5.0 - hv))


def _retention_decays(lg, c, n_tok):
    ri = lax.broadcasted_iota(jnp.int32, (c, c), 0).astype(F32)
    ci = lax.broadcasted_iota(jnp.int32, (c, c), 1).astype(F32)
    diff = ri - ci
    dmat = jnp.where(diff >= 0, jnp.exp(lg * jnp.maximum(diff, 0.0)), 0.0)
    idx = lax.broadcasted_iota(jnp.int32, (c, 1), 0).astype(F32)
    q_dec = jnp.exp(lg * (idx + 1.0))
    k_dec = jnp.exp(lg * (n_tok - 1.0 - idx))
    chunk_dec = jnp.exp(lg * float(n_tok))
    return dmat, q_dec, k_dec, chunk_dec


def _retention_chunk(q, k, v, state, decays):
    dmat, q_dec, k_dec, chunk_dec = decays
    inner = lax.dot_general(q, k, NT_DIMS, preferred_element_type=F32) * dmat
    o = jnp.dot(inner, v, preferred_element_type=F32)
    o = o + jnp.dot(q, state, preferred_element_type=F32) * q_dec
    new_state = state * chunk_dec + lax.dot_general(k * k_dec, v, TN_DIMS, preferred_element_type=F32)
    return o, new_state


def _group_norm_gate(o, gate, g_row):
    mu = jnp.mean(o, axis=-1, keepdims=True)
    d = o - mu
    var = jnp.mean(d * d, axis=-1, keepdims=True)
    normed = d * lax.rsqrt(var + GN_EPS) * g_row
    return (gate * _sigmoid(gate)) * normed


def _ret_prompt_body(q_ref, k_ref, v_ref, g_ref, cos_ref, sin_ref, gr_ref, o_ref, s_ref, *, chunk, kscale):
    h = pl.program_id(1)
    seq = q_ref.shape[0]
    decays = _retention_decays(_log_gamma(h, (1, 1)), chunk, chunk)
    state = jnp.zeros(s_ref.shape[2:], F32)
    for c in range(seq // chunk):
        rows = slice(c * chunk, (c + 1) * chunk)
        cos = cos_ref[rows, :]
        sin = sin_ref[rows, :]
        q = _rotate(q_ref[rows, :], cos, sin)
        k = _rotate(k_ref[rows, :], cos, sin) * kscale
        o, state = _retention_chunk(q, k, v_ref[rows, :], state, decays)
        o_ref[rows, :] = _group_norm_gate(o, g_ref[rows, :], gr_ref[0]).astype(o_ref.dtype)
    s_ref[0, 0] = state


def _ret_prompt(zr, cos, sin, g_ret, bsz, seq, n_heads, dk, dv):
    chunk = 256
    assert (2 * n_heads * dk) % dv == 0
    kb = n_heads
    vb = 2 * n_heads * dk // dv
    est = 2 * (2 * _nbytes((seq, dk), F32) + 3 * _nbytes((seq, dv), F32)) + 16 * _nbytes((chunk, dk), F32) \
        + 6 * _nbytes((dk, dv), F32)
    return pl.pallas_call(
        functools.partial(_ret_prompt_body, chunk=chunk, kscale=dk ** -0.5),
        grid=(bsz, n_heads),
        in_specs=[
            pl.BlockSpec((seq, dk), lambda b, h: (b, h)),
            pl.BlockSpec((seq, dk), lambda b, h: (b, kb + h)),
            pl.BlockSpec((seq, dv), lambda b, h: (b, vb + h)),
            pl.BlockSpec((seq, dv), lambda b, h: (b, vb + n_heads + h)),
            pl.BlockSpec((seq, dk // 2), lambda b, h: (0, 0)),
            pl.BlockSpec((seq, dk // 2), lambda b, h: (0, 0)),
            pl.BlockSpec((1, 1, dv), lambda b, h: (h, 0, 0)),
        ],
        out_specs=[
            pl.BlockSpec((seq, dv), lambda b, h: (b, h)),
            pl.BlockSpec((1, 1, dk, dv), lambda b, h: (b, h, 0, 0)),
        ],
        out_shape=[
            jax.ShapeDtypeStruct((bsz * seq, n_heads * dv), BF16),
            jax.ShapeDtypeStruct((bsz, n_heads, dk, dv), F32),
        ],
        compiler_params=_params(("arbitrary", "arbitrary"), est),
        name="retention_prompt",
    )(zr, zr, zr, zr, cos, sin, g_ret.reshape(n_heads, 1, dv))


def _ret_sample_body(z_ref, st_ref, cos_ref, sin_ref, gr_ref, o_ref, so_ref, *, n_heads, dk, dv, n_tok, kscale):
    cos = cos_ref[:n_tok, :]
    sin = sin_ref[:n_tok, :]
    kw = n_heads * dk
    vw = n_heads * dv
    for h in range(n_heads):
        q = _rotate(z_ref[0, :, h * dk:(h + 1) * dk], cos, sin)
        k = _rotate(z_ref[0, :, kw + h * dk:kw + (h + 1) * dk], cos, sin) * kscale
        v = z_ref[0, :, 2 * kw + h * dv:2 * kw + (h + 1) * dv]
        gate = z_ref[0, :, 2 * kw + vw + h * dv:2 * kw + vw + (h + 1) * dv]
        decays = _retention_decays(_log_gamma(h, (1, 1)), n_tok, n_tok)
        o, new_state = _retention_chunk(q, k, v, st_ref[0, h], decays)
        so_ref[0, h] = new_state
        o_ref[0, :, h * dv:(h + 1) * dv] = _group_norm_gate(o, gate, gr_ref[h:h + 1, :]).astype(o_ref.dtype)


def _ret_sample(zr3, state, cos, sin, g_ret):
    dbsz, n_tok, width = zr3.shape
    _, n_heads, dk, dv = state.shape
    est = 4 * _nbytes((n_heads, dk, dv), F32) + 4 * _nbytes((8, width), F32) + 4 * _nbytes((dk, dv), F32)
    return pl.pallas_call(
        functools.partial(_ret_sample_body, n_heads=n_heads, dk=dk, dv=dv, n_tok=n_tok, kscale=dk ** -0.5),
        grid=(dbsz,),
        in_specs=[
            pl.BlockSpec((1, n_tok, width), lambda b: (b, 0, 0)),
            pl.BlockSpec((1, n_heads, dk, dv), lambda b: (b, 0, 0, 0)),
            pl.BlockSpec(cos.shape, lambda b: (0, 0)),
            pl.BlockSpec(sin.shape, lambda b: (0, 0)),
            pl.BlockSpec((n_heads, dv), lambda b: (0, 0)),
        ],
        out_specs=[
            pl.BlockSpec((1, n_tok, n_heads * dv), lambda b: (b, 0, 0)),
            pl.BlockSpec((1, n_heads, dk, dv), lambda b: (b, 0, 0, 0)),
        ],
        out_shape=[
            jax.ShapeDtypeStruct((dbsz, n_tok, n_heads * dv), BF16),
            jax.ShapeDtypeStruct(state.shape, F32),
        ],
        compiler_params=_params(("arbitrary",), est),
        name="retention_sample",
    )(zr3, state, cos, sin, g_ret)


def _suffix_body(pt_ref, lf_hbm, lfn_ref, suf_ref, cs_ref, buf, sem, *, dbsz, n_pages, n_heads, page):
    b = pl.program_id(0)

    def page_copy(seq_idx, p):
        return pltpu.make_async_copy(lf_hbm.at[pt_ref[seq_idx, p]], buf.at[seq_idx * n_pages + p], sem.at[seq_idx])

    @pl.when(b == 0)
    def _():
        def start(r, carry):
            page_copy(r // n_pages, r % n_pages).start()
            return carry

        lax.fori_loop(0, dbsz * n_pages, start, 0)

    def wait(p, carry):
        page_copy(b, p).wait()
        return carry

    lax.fori_loop(0, n_pages, wait, 0)

    r_i = lax.broadcasted_iota(jnp.int32, (page, 2 * page), 0)
    c_i = lax.broadcasted_iota(jnp.int32, (page, 2 * page), 1)
    sel = jnp.where(c_i < page, (r_i > c_i).astype(F32), 1.0)
    rows = n_pages * n_heads
    x = buf[pl.ds(b * n_pages, n_pages)].reshape(rows, page)
    both = jnp.dot(x, sel, precision=lax.Precision.HIGHEST, preferred_element_type=F32)
    total = both[:, page:]
    later = total
    step = n_heads
    while step < rows:
        later = later + jnp.concatenate([later[step:], jnp.zeros((step, page), F32)], axis=0)
        step *= 2
    suf_ref[...] = (both[:, :page] + (later - total)).reshape(n_pages, n_heads, page)

    x_new = lfn_ref[0]
    acc = [x_new[0:1, :]]
    for t in range(1, x_new.shape[0]):
        acc.append(acc[-1] + x_new[t:t + 1, :])
    cs_ref[0] = jnp.concatenate(acc, axis=0)


def _sample_forget_bias(page_table, cache_lf_t, lf_new):
    dbsz, n_pages = page_table.shape
    _, n_heads, page = cache_lf_t.shape
    n_tok = lf_new.shape[1]
    n_rows = dbsz * n_pages
    grid_spec = pltpu.PrefetchScalarGridSpec(
        num_scalar_prefetch=1,
        grid=(dbsz,),
        in_specs=[
            pl.BlockSpec(memory_space=pl.ANY),
            pl.BlockSpec((1, n_tok, n_heads), lambda b, pt: (b, 0, 0)),
        ],
        out_specs=[
            pl.BlockSpec((n_pages, n_heads, page), lambda b, pt: (b, 0, 0)),
            pl.BlockSpec((1, n_tok, n_heads), lambda b, pt: (b, 0, 0)),
        ],
        scratch_shapes=[pltpu.VMEM((n_rows, n_heads, page), F32), pltpu.SemaphoreType.DMA((dbsz,))],
    )
    est = _nbytes((n_rows, n_heads, page), F32) + 12 * _nbytes((n_pages * n_heads, 2 * page), F32)
    return pl.pallas_call(
        functools.partial(_suffix_body, dbsz=dbsz, n_pages=n_pages, n_heads=n_heads, page=page),
        grid_spec=grid_spec,
        out_shape=[
            jax.ShapeDtypeStruct((n_rows, n_heads, page), F32),
            jax.ShapeDtypeStruct((dbsz, n_tok, n_heads), F32),
        ],
        compiler_params=_params(("arbitrary",), est),
        name="sample_forget_bias",
    )(page_table, cache_lf_t, lf_new)


def _up_attn_body(pt_ref, a_ref, b_ref, q_ref, kn_ref, vn_ref, bias_ref, csc_ref, bn_ref, *rest,
                  n_pg, n_groups, n_heads, scale):
    k_refs = rest[:n_pg]
    v_refs = rest[n_pg:2 * n_pg]
    up_ref, o_ref = rest[2 * n_pg:2 * n_pg + 2]
    rowbias_ref, m_ref, l_ref, acc_ref = rest[2 * n_pg + 2:]

    j = (pl.program_id(0) * pl.num_programs(1) + pl.program_id(1)) % n_groups
    rows, cols = rowbias_ref.shape

    @pl.when(j == 0)
    def _():
        rh = lax.broadcasted_iota(jnp.int32, (rows, cols), 0) % n_heads
        ch = lax.broadcasted_iota(jnp.int32, (rows, cols), 1) % n_heads
        rowbias_ref[...] = jnp.where(rh == ch, csc_ref[0][:, :1], -jnp.inf)
        m_ref[...] = jnp.full_like(m_ref, -jnp.inf)
        l_ref[...] = jnp.zeros_like(l_ref)
        acc_ref[...] = jnp.zeros_like(acc_ref)

    q = q_ref[0]
    rowbias = rowbias_ref[...]
    s_pages = []
    for i in range(n_pg):
        s = lax.dot_general(q, k_refs[i][0], NT_DIMS, preferred_element_type=F32) * scale
        s_pages.append(s + (rowbias + bias_ref[0, 0, i:i + 1, :]))
    up = jnp.dot(a_ref[...], b_ref[...], preferred_element_type=F32)
    up_ref[...] = jnp.square(jnp.maximum(up, 0.0)).astype(up_ref.dtype)

    m_old = m_ref[...]
    m_new = m_old
    for s in s_pages:
        m_new = jnp.maximum(m_new, jnp.max(s, axis=1, keepdims=True))
    alpha = jnp.exp(m_old - m_new)
    l_new = alpha * l_ref[...]
    acc_new = alpha * acc_ref[...]
    for i in range(n_pg):
        p = jnp.exp(s_pages[i] - m_new)
        l_new = l_new + jnp.sum(p, axis=1, keepdims=True)
        acc_new = acc_new + jnp.dot(p, v_refs[i][0], preferred_element_type=F32)
    m_ref[...] = m_new
    l_ref[...] = l_new
    acc_ref[...] = acc_new

    @pl.when(j == n_groups - 1)
    def _():
        sn = lax.dot_general(q, kn_ref[0], NT_DIMS, preferred_element_type=F32) * scale + bn_ref[0]
        m2 = jnp.maximum(m_new, jnp.max(sn, axis=1, keepdims=True))
        a2 = jnp.exp(m_new - m2)
        p2 = jnp.exp(sn - m2)
        l2 = a2 * l_new + jnp.sum(p2, axis=1, keepdims=True)
        o = (a2 * acc_new + jnp.dot(p2, vn_ref[0], preferred_element_type=F32)) / l2
        o_ref[0] = o.astype(o_ref.dtype)


MAX_PAGES_PER_STEP = 4


def _mlp_up_with_sample_attention(h2, w_up, page_table, q3, kn_pad, vn_pad, bias_rows, cs_col, bias_new,
                                  cache_k, cache_v, n_heads, tn):
    m, kdim = h2.shape
    n = w_up.shape[1]
    dbsz, rows, head_dim = q3.shape
    n_pages = page_table.shape[1]
    cols = cache_k.shape[1]
    pad_rows = kn_pad.shape[1]
    total_pages = dbsz * n_pages
    tm = min(m, 1024)
    while total_pages // ((n // tn) * (m // tm)) > MAX_PAGES_PER_STEP:
        tm //= 2
    gj, gi = n // tn, m // tm
    assert m % tm == 0 and tm % 16 == 0 and n % tn == 0 and total_pages % (gj * gi) == 0
    n_pg = total_pages // (gj * gi)
    assert n_pages % n_pg == 0
    n_groups = n_pages // n_pg

    def seq_of(j, i):
        return (j * gi + i) // n_groups

    def grp_of(j, i):
        return (j * gi + i) % n_groups

    def per_seq(block):
        return pl.BlockSpec(block, lambda j, i, pt: (seq_of(j, i), 0, 0))

    def page_spec(idx):
        return pl.BlockSpec((1, cols, head_dim),
                            lambda j, i, pt, idx=idx: (pt[seq_of(j, i), grp_of(j, i) * n_pg + idx], 0, 0))

    grid_spec = pltpu.PrefetchScalarGridSpec(
        num_scalar_prefetch=1,
        grid=(gj, gi),
        in_specs=[
            pl.BlockSpec((tm, kdim), lambda j, i, pt: (i, 0)),
            pl.BlockSpec((kdim, tn), lambda j, i, pt: (0, j)),
            per_seq((1, rows, head_dim)),
            per_seq((1, pad_rows, head_dim)),
            per_seq((1, pad_rows, head_dim)),
            pl.BlockSpec((1, 1, n_pg, cols), lambda j, i, pt: (seq_of(j, i), grp_of(j, i), 0, 0)),
            per_seq((1, rows, LANES)),
            per_seq((1, rows, LANES)),
        ] + [page_spec(idx) for idx in range(n_pg)] * 2,
        out_specs=[
            pl.BlockSpec((tm, tn), lambda j, i, pt: (i, j)),
            per_seq((1, rows, head_dim)),
        ],
        scratch_shapes=[
            pltpu.VMEM((rows, cols), F32),
            pltpu.VMEM((rows, 1), F32),
            pltpu.VMEM((rows, 1), F32),
            pltpu.VMEM((rows, head_dim), F32),
        ],
    )
    est = 2 * (_nbytes((tm, kdim), h2.dtype) + _nbytes((kdim, tn), w_up.dtype) + _nbytes((tm, tn), BF16)) \
        + 2 * _nbytes((tm, tn), F32) + 4 * n_pg * _nbytes((cols, head_dim), F32) \
        + (3 * n_pg + 4) * _nbytes((rows, cols), F32)
    return pl.pallas_call(
        functools.partial(_up_attn_body, n_pg=n_pg, n_groups=n_groups, n_heads=n_heads, scale=head_dim ** -0.5),
        grid_spec=grid_spec,
        out_shape=[
            jax.ShapeDtypeStruct((m, n), BF16),
            jax.ShapeDtypeStruct((dbsz, rows, head_dim), BF16),
        ],
        compiler_params=_params(("arbitrary", "arbitrary"), est),
        name="mlp_up_fox_sample",
    )(page_table, h2, w_up, q3, kn_pad, vn_pad, bias_rows.reshape(dbsz, n_groups, n_pg, cols), cs_col, bias_new,
      *([cache_k] * n_pg), *([cache_v] * n_pg))


def _mix_inputs(x, w, lw, tm):
    d = x.shape[1]
    fox_w = lw["fox_w"]
    h, lf_pad = _norm_fgate(x, w["g_mix"], lw["wf_rows"], lw["bf_pad"], tm)
    tn = 512
    qkv = [
        _mm("in_proj_fox", [h], [lw["w_in_t"]], fox_w, b_offs=[o * fox_w], b_rows=[True], epi=_epi_id,
            out_dtype=F32, tm=tm, tn=tn)
        for o in range(3)
    ]
    zr = _mm("in_proj_ret", [h], [lw["w_in_t"]], lw["ret_w"], b_offs=[lw["ret_off"]], b_rows=[True], epi=_epi_id,
             out_dtype=F32, tm=tm, tn=tn)
    gates = _mm("merge_gates", [h], [w["w_gate"]], 2 * d, extras=[(w["b_gate"].reshape(1, 2 * d), 0)],
                epi=_epi_sigmoid_bias, out_dtype=F32, tm=tm, tn=tn)
    return lf_pad, qkv, zr, gates


def _merge_branches(x, oa, ob, gates, w, tm):
    d = x.shape[1]
    tn = 512
    merged = _mm("branch_merge", [oa, ob], [w["w_pa"], w["w_pb"]], d, extras=[(gates, 0), (gates, d)],
                 epi=_epi_merge, out_dtype=BF16, tm=tm, tn=tn)
    x1 = _mm("out_proj", [merged], [w["w_o"]], d, extras=[(x, 0)], epi=_epi_residual, out_dtype=F32, tm=tm, tn=tn)
    return x1, _rmsnorm(x1, w["g_mlp"], BF16, min(tm, 512))


def _finish(x1, up, p, w, tm, last):
    d = x1.shape[1]
    tn = 512
    wide = up.shape[1] > 4096 and d % 1024 == 0
    x2 = _mm("mlp_down", [up], [w["w_down"]], d, extras=[(x1, 0)], epi=_epi_residual, out_dtype=F32,
             tm=tm, tn=1024 if wide else tn, tk=2048 if wide else None)
    h3 = _rmsnorm(x2, w["g_ple"], BF16, min(tm, 512))
    x3 = _mm("ple", [h3, p], [w["w_ple_gate"], w["w_ple"]], d, extras=[(x2, 0)], epi=_epi_ple, out_dtype=F32,
             tm=tm, tn=tn)
    if last is not None:
        return _rmsnorm(x3, last, F32, min(tm, 512))
    return x3


def kernel(x_prompt, x_sample, cache_k, cache_v, cache_logf, state_ret, page_table, p_prompt, p_sample, g_mix, w_in,
           b_f, g_ret, w_pa, w_pb, w_gate, b_gate, w_o, g_mlp, w_up, w_down, g_ple, w_ple_gate, w_ple, g_final):
    bsz, seq, d = x_prompt.shape
    dbsz, dseq, _ = x_sample.shape
    depth, n_phys, page, ha, da = cache_k.shape
    _, _, hr, dk, dv = state_ret.shape
    n_pages = page_table.shape[1]
    past = n_pages * page
    fox_w = ha * da
    f_off = 3 * fox_w
    tp, ts = bsz * seq, dbsz * dseq
    rows = dseq * ha

    cos_p, sin_p = _rope_tables(seq, dk // 2, 0)
    cos_s, sin_s = _rope_tables(8, dk // 2, past)

    xp = x_prompt.reshape(tp, d)
    xs = x_sample.reshape(ts, d)
    outs = [[] for _ in range(8)]
    for i in range(depth):
        w = dict(g_mix=g_mix[i], w_pa=w_pa[i], w_pb=w_pb[i], w_gate=w_gate[i], b_gate=b_gate[i],
                 w_o=w_o[i], g_mlp=g_mlp[i], w_up=w_up[i], w_down=w_down[i], g_ple=g_ple[i],
                 w_ple_gate=w_ple_gate[i], w_ple=w_ple[i])
        w_in_t = jnp.swapaxes(w_in[i], 0, 1)
        lw = dict(
            fox_w=fox_w,
            w_in_t=w_in_t,
            wf_rows=jnp.pad(w_in_t[f_off:f_off + ha], ((0, LANES - ha), (0, 0))),
            bf_pad=jnp.pad(b_f[i].reshape(1, ha), ((0, 0), (0, LANES - ha))),
            ret_off=f_off + ha,
            ret_w=w_in_t.shape[0] - f_off - ha,
        )
        last = g_final if i == depth - 1 else None

        lf_pad, (q, k, v), zr, gates = _mix_inputs(xp, w, lw, 1024)
        ct = _prompt_cumsum(lf_pad, bsz, seq, ha)
        oa = _fox_prompt(q, k, v, ct, bsz, seq, ha, da)
        ob, sp = _ret_prompt(zr, cos_p, sin_p, g_ret[i], bsz, seq, hr, dk, dv)
        x1p, h2p = _merge_branches(xp, oa, ob, gates, w, 1024)
        outs[0].append(k.reshape(bsz, seq, ha, da))
        outs[1].append(v.reshape(bsz, seq, ha, da))
        outs[2].append(lf_pad[:, :ha].reshape(bsz, seq, ha))
        outs[3].append(sp)

        lf_pad, (q, k, v), zr, gates = _mix_inputs(xs, w, lw, ts)
        lf_new = lf_pad[:, :ha].reshape(dbsz, dseq, ha)
        suf, cs = _sample_forget_bias(page_table, jnp.swapaxes(cache_logf[i], 1, 2), lf_new)
        bias_rows = jnp.swapaxes(suf.reshape(dbsz, n_pages, ha, page), 2, 3).reshape(dbsz, n_pages, page * ha)
        cs_col = jnp.broadcast_to(cs.reshape(dbsz, rows, 1), (dbsz, rows, LANES))
        diff = cs[:, :, :, None, None] - cs[:, None, None, :, :]
        tt = jnp.arange(dseq)
        hh = jnp.arange(ha)
        keep = (tt[None, None, :, None] <= tt[:, None, None, None]) & (hh[None, :, None, None] == hh[None, None, None, :])
        bias_new = jnp.where(keep[None], diff, -jnp.inf).reshape(dbsz, rows, rows)
        bias_new = jnp.pad(bias_new, ((0, 0), (0, 0), (0, LANES - rows)), constant_values=-jnp.inf)
        pad_new = lambda t: jnp.pad(t.reshape(dbsz, rows, da), ((0, 0), (0, LANES - rows), (0, 0)))
        up_p, oa = _mlp_up_with_sample_attention(
            h2p, w["w_up"], page_table, q.reshape(dbsz, rows, da), pad_new(k), pad_new(v), bias_rows, cs_col,
            bias_new, cache_k[i].reshape(n_phys, page * ha, da), cache_v[i].reshape(n_phys, page * ha, da), ha, 512)
        xp = _finish(x1p, up_p, p_prompt[i].reshape(tp, -1), w, 1024, last)

        ob, ss = _ret_sample(zr.reshape(dbsz, dseq, -1), state_ret[i], cos_s, sin_s, g_ret[i])
        x1s, h2s = _merge_branches(xs, oa.reshape(ts, fox_w), ob.reshape(ts, hr * dv), gates, w, ts)
        up_s = _mm("mlp_up", [h2s], [w["w_up"]], w["w_up"].shape[1], epi=_epi_relu2, out_dtype=BF16, tm=ts, tn=512)
        xs = _finish(x1s, up_s, p_sample[i].reshape(ts, -1), w, ts, last)
        outs[4].append(k.reshape(dbsz, dseq, ha, da))
        outs[5].append(v.reshape(dbsz, dseq, ha, da))
        outs[6].append(lf_new)
        outs[7].append(ss)

    return (xp.reshape(bsz, seq, d), xs.reshape(dbsz, dseq, d), *[jnp.stack(o) for o in outs])
```

```python
import functools

import jax
import jax.numpy as jnp
from jax import lax
from jax.experimental import pallas as pl
from jax.experimental.pallas import tpu as pltpu

F32 = jnp.float32
BF16 = jnp.bfloat16

NORM_EPS = 1e-6
GN_EPS = 1e-5
ROPE_BASE = 10000.0
LOG2E = 1.4426950408889634

LANES = 128
VMEM_CAP_BYTES = 60 * 1024 * 1024
VMEM_SLACK_BYTES = 6 * 1024 * 1024

NT_DIMS = (((1,), (1,)), ((), ()))
TN_DIMS = (((0,), (0,)), ((), ()))


def _nbytes(shape, dtype):
    n = 1
    for s in shape:
        n *= s
    return n * jnp.dtype(dtype).itemsize


def _params(semantics, est_bytes):
    limit = min(VMEM_CAP_BYTES, max(32 * 1024 * 1024, est_bytes + VMEM_SLACK_BYTES))
    return pltpu.CompilerParams(dimension_semantics=semantics, vmem_limit_bytes=limit)


def _sigmoid(x):
    return 1.0 / (1.0 + jnp.exp(-x))


def _log_sigmoid(x):
    return -(jnp.maximum(-x, 0.0) + jnp.log1p(jnp.exp(-jnp.abs(x))))


def _rmsnorm_body(x_ref, g_ref, o_ref):
    x = x_ref[...]
    var = jnp.mean(x * x, axis=-1, keepdims=True)
    o_ref[...] = (x * lax.rsqrt(var + NORM_EPS) * g_ref[...]).astype(o_ref.dtype)


def _rmsnorm(x, g, out_dtype, tm):
    m, d = x.shape
    est = 2 * (_nbytes((tm, d), x.dtype) + _nbytes((tm, d), out_dtype)) + _nbytes((tm, d), F32)
    return pl.pallas_call(
        _rmsnorm_body,
        grid=(m // tm,),
        in_specs=[pl.BlockSpec((tm, d), lambda i: (i, 0)), pl.BlockSpec((1, d), lambda i: (0, 0))],
        out_specs=pl.BlockSpec((tm, d), lambda i: (i, 0)),
        out_shape=jax.ShapeDtypeStruct((m, d), out_dtype),
        compiler_params=_params(("arbitrary",), est),
        name="rmsnorm",
    )(x, g.reshape(1, d))


def _norm_fgate_body(x_ref, g_ref, wf_ref, bf_ref, h_ref, lf_ref):
    x = x_ref[...]
    var = jnp.mean(x * x, axis=-1, keepdims=True)
    h = x * lax.rsqrt(var + NORM_EPS) * g_ref[...]
    h_ref[...] = h.astype(h_ref.dtype)
    fa = lax.dot_general(h, wf_ref[...], NT_DIMS, preferred_element_type=F32) + bf_ref[...]
    lf_ref[...] = _log_sigmoid(fa)


def _norm_fgate(x, g, wf_rows, bf_pad, tm):
    m, d = x.shape
    est = 2 * (_nbytes((tm, d), F32) + _nbytes((tm, d), BF16) + _nbytes((LANES, d), F32)) + 2 * _nbytes((tm, d), F32)
    return pl.pallas_call(
        _norm_fgate_body,
        grid=(m // tm,),
        in_specs=[
            pl.BlockSpec((tm, d), lambda i: (i, 0)),
            pl.BlockSpec((1, d), lambda i: (0, 0)),
            pl.BlockSpec((LANES, d), lambda i: (0, 0)),
            pl.BlockSpec((1, LANES), lambda i: (0, 0)),
        ],
        out_specs=[pl.BlockSpec((tm, d), lambda i: (i, 0)), pl.BlockSpec((tm, LANES), lambda i: (i, 0))],
        out_shape=[jax.ShapeDtypeStruct((m, d), BF16), jax.ShapeDtypeStruct((m, LANES), F32)],
        compiler_params=_params(("arbitrary",), est),
        name="norm_fgate",
    )(x, g.reshape(1, d), wf_rows, bf_pad)


def _mm_body(*refs, n_dots, n_extra, n_rider_extra, epi, k_steps, b_rows):
    it = iter(refs)
    a_refs = [next(it) for _ in range(n_dots)]
    b_refs = [next(it) for _ in range(n_dots)]
    ex_refs = [next(it) for _ in range(n_extra)]
    has_rider = n_rider_extra is not None
    ra_refs = [next(it) for _ in range(n_dots)] if has_rider else []
    rex_refs = [next(it) for _ in range(n_rider_extra)] if has_rider else []
    o_ref = next(it)
    ro_ref = next(it) if has_rider else None

    def dot(a, b, rows):
        if rows:
            return lax.dot_general(a, b, NT_DIMS, preferred_element_type=F32)
        return jnp.dot(a, b, preferred_element_type=F32)

    def tile(a_rs, ex_rs, out_ref):
        if k_steps == 1:
            accs = [dot(a[...], b[...], r) for a, b, r in zip(a_rs, b_refs, b_rows)]
            out_ref[...] = epi(accs, [e[...] for e in ex_rs]).astype(out_ref.dtype)
            return
        k = pl.program_id(2)
        part = dot(a_rs[0][...], b_refs[0][...], b_rows[0])

        @pl.when(k == 0)
        def _():
            out_ref[...] = ex_rs[0][...] + part

        @pl.when(k > 0)
        def _():
            out_ref[...] += part

    tile(a_refs, ex_refs, o_ref)
    if has_rider:
        @pl.when(pl.program_id(1) == 0)
        def _():
            tile(ra_refs, rex_refs, ro_ref)


def _mm(name, a_list, b_list, n, *, epi, out_dtype, tm, tn, b_offs=None, b_rows=None, extras=(), tk=None,
        rider=None):
    m = a_list[0].shape[0]
    n_dots = len(a_list)
    b_offs = list(b_offs) if b_offs is not None else [0] * n_dots
    b_rows = tuple(b_rows) if b_rows is not None else (False,) * n_dots
    assert m % tm == 0 and n % tn == 0
    assert all(o % tn == 0 or (rows and o % 8 == 0) for o, rows in zip(b_offs, b_rows))
    k0 = a_list[0].shape[1]
    k_steps = 1 if tk is None else k0 // tk
    if k_steps > 1:
        assert n_dots == 1 and k0 % tk == 0 and not b_rows[0] and epi is _epi_residual and out_dtype == F32

    est = 0
    if k_steps == 1:
        grid = (n // tn, m // tm)
        sem = ("arbitrary", "arbitrary")
        a_specs = [pl.BlockSpec((tm, a.shape[1]), lambda j, i: (i, 0)) for a in a_list]
        b_specs = []
        for b, o, rows in zip(b_list, b_offs, b_rows):
            if rows and o % tn:
                b_specs.append(pl.BlockSpec((pl.Element(tn), pl.Element(b.shape[1])),
                                            lambda j, i, o=o: (pl.multiple_of(o + j * tn, 8), 0)))
            elif rows:
                b_specs.append(pl.BlockSpec((tn, b.shape[1]), lambda j, i, o=o // tn: (j + o, 0)))
            else:
                b_specs.append(pl.BlockSpec((b.shape[0], tn), lambda j, i, o=o // tn: (0, j + o)))
        for a in a_list:
            est += 2 * (_nbytes((tm, a.shape[1]), a.dtype) + _nbytes((a.shape[1], tn), F32))
        scratch = []
    else:
        grid = (n // tn, m // tm, k_steps)
        sem = ("arbitrary", "arbitrary", "arbitrary")
        a_specs = [pl.BlockSpec((tm, tk), lambda j, i, k: (i, k))]
        b_specs = [pl.BlockSpec((tk, tn), lambda j, i, k, o=b_offs[0] // tn: (k, j + o))]
        est += 2 * (_nbytes((tm, tk), a_list[0].dtype) + _nbytes((tk, tn), b_list[0].dtype))
        scratch = []
    ex_specs = []
    for arr, off in extras:
        assert off % tn == 0
        rows = 1 if arr.shape[0] == 1 else tm
        if k_steps == 1:
            imap = (lambda j, i, o=off // tn: (0, j + o)) if rows == 1 else (lambda j, i, o=off // tn: (i, j + o))
        else:
            imap = (lambda j, i, k, o=off // tn: (0, j + o)) if rows == 1 else (lambda j, i, k, o=off // tn: (i, j + o))
        ex_specs.append(pl.BlockSpec((rows, tn), imap))
        est += 2 * _nbytes((rows, tn), arr.dtype)
    out_map = (lambda j, i: (i, j)) if k_steps == 1 else (lambda j, i, k: (i, j))
    est += 2 * _nbytes((tm, tn), out_dtype) + (n_dots + 2) * _nbytes((tm, tn), F32)
    in_specs = a_specs + b_specs + ex_specs
    operands = [*a_list, *b_list, *[arr for arr, _ in extras]]
    out_specs = pl.BlockSpec((tm, tn), out_map)
    out_shape = jax.ShapeDtypeStruct((m, n), out_dtype)
    n_rider_extra = None
    if rider is not None:
        r_a_list, r_extras = rider
        ms = r_a_list[0].shape[0]
        n_rider_extra = len(r_extras)
        for a in r_a_list:
            if k_steps == 1:
                in_specs.append(pl.BlockSpec((ms, a.shape[1]), lambda j, i: (0, 0)))
            else:
                in_specs.append(pl.BlockSpec((ms, tk), lambda j, i, k: (0, k)))
            est += 2 * _nbytes((ms, a.shape[1] if k_steps == 1 else tk), a.dtype)
        for arr, off in r_extras:
            assert off % tn == 0 and arr.shape[0] == ms
            imap = (lambda j, i, o=off // tn: (0, j + o)) if k_steps == 1 else (lambda j, i, k, o=off // tn: (0, j + o))
            in_specs.append(pl.BlockSpec((ms, tn), imap))
        operands += [*r_a_list, *[arr for arr, _ in r_extras]]
        r_out_map = (lambda j, i: (0, j)) if k_steps == 1 else (lambda j, i, k: (0, j))
        out_specs = [out_specs, pl.BlockSpec((ms, tn), r_out_map)]
        out_shape = [out_shape, jax.ShapeDtypeStruct((ms, n), out_dtype)]
        est += 4 * _nbytes((ms, tn), F32)
    body = functools.partial(_mm_body, n_dots=n_dots, n_extra=len(extras), n_rider_extra=n_rider_extra, epi=epi,
                             k_steps=k_steps, b_rows=b_rows)
    return pl.pallas_call(
        body,
        grid=grid,
        in_specs=in_specs,
        out_specs=out_specs,
        out_shape=out_shape,
        scratch_shapes=scratch,
        compiler_params=_params(sem, est),
        name=name,
    )(*operands)


def _epi_id(accs, ex):
    return accs[0]


def _epi_sigmoid_bias(accs, ex):
    return _sigmoid(accs[0] + ex[0])


def _epi_merge(accs, ex):
    return ex[0] * accs[0] + ex[1] * accs[1]


def _epi_residual(accs, ex):
    return ex[0] + accs[0]


def _epi_relu2(accs, ex):
    return jnp.square(jnp.maximum(accs[0], 0.0))


def _epi_ple(accs, ex):
    return ex[0] + _sigmoid(accs[0]) * accs[1]


def _rope_body(cos_ref, sin_ref, *, base_pos):
    n, half = cos_ref.shape
    pos = (lax.broadcasted_iota(jnp.int32, (n, half), 0) + base_pos).astype(F32)
    idx = lax.broadcasted_iota(jnp.int32, (n, half), 1).astype(F32)
    inv = ROPE_BASE ** (-idx / half)
    ang = pos * inv
    cos_ref[...] = jnp.cos(ang)
    sin_ref[...] = jnp.sin(ang)


def _rope_tables(n_rows, half, base_pos):
    return pl.pallas_call(
        functools.partial(_rope_body, base_pos=base_pos),
        out_shape=[jax.ShapeDtypeStruct((n_rows, half), F32)] * 2,
        name="rope_tables",
    )()


def _rotate(x, cos, sin):
    half = x.shape[-1] // 2
    x1 = x[:, :half]
    x2 = x[:, half:]
    return jnp.concatenate([x1 * cos - x2 * sin, x2 * cos + x1 * sin], axis=1)


def _cumsum_body(lf_ref, ct_ref, *, blk):
    s_len = lf_ref.shape[0]
    n_heads = ct_ref.shape[1]
    r = lax.broadcasted_iota(jnp.int32, (blk, blk), 0)
    c = lax.broadcasted_iota(jnp.int32, (blk, blk), 1)
    upper = (r <= c).astype(F32)
    carry = jnp.zeros((LANES, 1), F32)
    for j in range(s_len // blk):
        x = lf_ref[j * blk:(j + 1) * blk, :]
        loc = lax.dot_general(x, upper, TN_DIMS, precision=lax.Precision.HIGHEST,
                              preferred_element_type=F32) + carry
        ct_ref[0, :, j * blk:(j + 1) * blk] = loc[:n_heads, :]
        carry = loc[:, blk - 1:blk]


def _prompt_cumsum(lf_pad, bsz, seq, n_heads):
    blk = 256
    return pl.pallas_call(
        functools.partial(_cumsum_body, blk=blk),
        grid=(bsz,),
        in_specs=[pl.BlockSpec((seq, LANES), lambda b: (b, 0))],
        out_specs=pl.BlockSpec((1, n_heads, seq), lambda b: (b, 0, 0)),
        out_shape=jax.ShapeDtypeStruct((bsz, n_heads, seq), F32),
        compiler_params=_params(("arbitrary",), 4 * _nbytes((seq, LANES), F32)),
        name="forget_cumsum",
    )(lf_pad)


def _fox_prompt_body(q_ref, k_ref, v_ref, c_ref, o_ref, *, tq, scale):
    seq = q_ref.shape[0]
    nsub = tq // LANES
    row = lax.broadcasted_iota(jnp.int32, (tq, tq), 0)
    col = lax.broadcasted_iota(jnp.int32, (tq, tq), 1)
    causal = col <= row
    c2 = c_ref[0, 0] * LOG2E
    for i in range(seq // tq):
        kv = (i + 1) * tq
        q = q_ref[i * tq:(i + 1) * tq, :] * (scale * LOG2E)
        cq = jnp.concatenate(
            [jnp.broadcast_to(c2[i * nsub + a:i * nsub + a + 1, :], (LANES, LANES)).T[:, :1]
             for a in range(nsub)], axis=0)
        ck = jnp.concatenate([c2[r:r + 1, :] for r in range(kv // LANES)], axis=1)
        s = lax.dot_general(q, k_ref[0:kv, :], NT_DIMS, preferred_element_type=F32)
        s = s + cq - ck
        diag = jnp.where(causal, s[:, i * tq:], -jnp.inf)
        s = diag if i == 0 else jnp.concatenate([s[:, :i * tq], diag], axis=1)
        m = jnp.max(s, axis=1, keepdims=True)
        p = jnp.exp2(s - m)
        l = jnp.sum(p, axis=1, keepdims=True)
        o = jnp.dot(p, v_ref[0:kv, :], preferred_element_type=F32)
        o_ref[i * tq:(i + 1) * tq, :] = (o / l).astype(o_ref.dtype)


def _fox_prompt(q, k, v, ct, bsz, seq, n_heads, head_dim):
    tq = 256
    c4 = ct.reshape(bsz, n_heads, seq // LANES, LANES)
    est = 8 * _nbytes((seq, head_dim), F32) + 6 * _nbytes((tq, seq), F32)
    blk = pl.BlockSpec((seq, head_dim), lambda b, h: (b, h))
    return pl.pallas_call(
        functools.partial(_fox_prompt_body, tq=tq, scale=head_dim ** -0.5),
        grid=(bsz, n_heads),
        in_specs=[blk, blk, blk, pl.BlockSpec((1, 1, seq // LANES, LANES), lambda b, h: (b, h, 0, 0))],
        out_specs=blk,
        out_shape=jax.ShapeDtypeStruct((bsz * seq, n_heads * head_dim), BF16),
        compiler_params=_params(("arbitrary", "arbitrary"), est),
        name="fox_prompt",
    )(q, k, v, c4)


def _log_gamma(h, shape):
    hv = jnp.full(shape, h, jnp.int32).astype(F32)
    return jnp.log(1.0 - jnp.exp2(-5.0 - hv))


def _retention_decays(lg, c, n_tok):
    ri = lax.broadcasted_iota(jnp.int32, (c, c), 0).astype(F32)
    ci = lax.broadcasted_iota(jnp.int32, (c, c), 1).astype(F32)
    diff = ri - ci
    dmat = jnp.where(diff >= 0, jnp.exp(lg * jnp.maximum(diff, 0.0)), 0.0)
    idx = lax.broadcasted_iota(jnp.int32, (c, 1), 0).astype(F32)
    q_dec = jnp.exp(lg * (idx + 1.0))
    k_dec = jnp.exp(lg * (n_tok - 1.0 - idx))
    chunk_dec = jnp.exp(lg * float(n_tok))
    return dmat, q_dec, k_dec, chunk_dec


def _retention_chunk(q, k, v, state, decays):
    dmat, q_dec, k_dec, chunk_dec = decays
    inner = lax.dot_general(q, k, NT_DIMS, preferred_element_type=F32) * dmat
    o = jnp.dot(inner, v, preferred_element_type=F32)
    o = o + jnp.dot(q, state, preferred_element_type=F32) * q_dec
    new_state = state * chunk_dec + lax.dot_general(k * k_dec, v, TN_DIMS, preferred_element_type=F32)
    return o, new_state


def _group_norm_gate(o, gate, g_row):
    mu = jnp.mean(o, axis=-1, keepdims=True)
    d = o - mu
    var = jnp.mean(d * d, axis=-1, keepdims=True)
    normed = d * lax.rsqrt(var + GN_EPS) * g_row
    return (gate * _sigmoid(gate)) * normed


def _ret_prompt_body(q_ref, k_ref, v_ref, g_ref, cos_ref, sin_ref, gr_ref, o_ref, s_ref, *, chunk, kscale):
    h = pl.program_id(1)
    seq = q_ref.shape[0]
    decays = _retention_decays(_log_gamma(h, (1, 1)), chunk, chunk)
    state = jnp.zeros(s_ref.shape[2:], F32)
    for c in range(seq // chunk):
        rows = slice(c * chunk, (c + 1) * chunk)
        cos = cos_ref[rows, :]
        sin = sin_ref[rows, :]
        q = _rotate(q_ref[rows, :], cos, sin)
        k = _rotate(k_ref[rows, :], cos, sin) * kscale
        o, state = _retention_chunk(q, k, v_ref[rows, :], state, decays)
        o_ref[rows, :] = _group_norm_gate(o, g_ref[rows, :], gr_ref[0]).astype(o_ref.dtype)
    s_ref[0, 0] = state


def _ret_prompt(zr, cos, sin, g_ret, bsz, seq, n_heads, dk, dv):
    chunk = 256
    assert (2 * n_heads * dk) % dv == 0
    kb = n_heads
    vb = 2 * n_heads * dk // dv
    est = 2 * (2 * _nbytes((seq, dk), F32) + 3 * _nbytes((seq, dv), F32)) + 16 * _nbytes((chunk, dk), F32) \
        + 6 * _nbytes((dk, dv), F32)
    return pl.pallas_call(
        functools.partial(_ret_prompt_body, chunk=chunk, kscale=dk ** -0.5),
        grid=(bsz, n_heads),
        in_specs=[
            pl.BlockSpec((seq, dk), lambda b, h: (b, h)),
            pl.BlockSpec((seq, dk), lambda b, h: (b, kb + h)),
            pl.BlockSpec((seq, dv), lambda b, h: (b, vb + h)),
            pl.BlockSpec((seq, dv), lambda b, h: (b, vb + n_heads + h)),
            pl.BlockSpec((seq, dk // 2), lambda b, h: (0, 0)),
            pl.BlockSpec((seq, dk // 2), lambda b, h: (0, 0)),
            pl.BlockSpec((1, 1, dv), lambda b, h: (h, 0, 0)),
        ],
        out_specs=[
            pl.BlockSpec((seq, dv), lambda b, h: (b, h)),
            pl.BlockSpec((1, 1, dk, dv), lambda b, h: (b, h, 0, 0)),
        ],
        out_shape=[
            jax.ShapeDtypeStruct((bsz * seq, n_heads * dv), BF16),
            jax.ShapeDtypeStruct((bsz, n_heads, dk, dv), F32),
        ],
        compiler_params=_params(("arbitrary", "arbitrary"), est),
        name="retention_prompt",
    )(zr, zr, zr, zr, cos, sin, g_ret.reshape(n_heads, 1, dv))


def _ret_sample_body(z_ref, st_ref, cos_ref, sin_ref, gr_ref, o_ref, so_ref, *, n_heads, dk, dv, n_tok, kscale):
    cos = cos_ref[:n_tok, :]
    sin = sin_ref[:n_tok, :]
    kw = n_heads * dk
    vw = n_heads * dv
    for h in range(n_heads):
        q = _rotate(z_ref[0, :, h * dk:(h + 1) * dk], cos, sin)
        k = _rotate(z_ref[0, :, kw + h * dk:kw + (h + 1) * dk], cos, sin) * kscale
        v = z_ref[0, :, 2 * kw + h * dv:2 * kw + (h + 1) * dv]
        gate = z_ref[0, :, 2 * kw + vw + h * dv:2 * kw + vw + (h + 1) * dv]
        decays = _retention_decays(_log_gamma(h, (1, 1)), n_tok, n_tok)
        o, new_state = _retention_chunk(q, k, v, st_ref[0, h], decays)
        so_ref[0, h] = new_state
        o_ref[0, :, h * dv:(h + 1) * dv] = _group_norm_gate(o, gate, gr_ref[h:h + 1, :]).astype(o_ref.dtype)


def _ret_sample(zr3, state, cos, sin, g_ret):
    dbsz, n_tok, width = zr3.shape
    _, n_heads, dk, dv = state.shape
    est = 4 * _nbytes((n_heads, dk, dv), F32) + 4 * _nbytes((8, width), F32) + 4 * _nbytes((dk, dv), F32)
    return pl.pallas_call(
        functools.partial(_ret_sample_body, n_heads=n_heads, dk=dk, dv=dv, n_tok=n_tok, kscale=dk ** -0.5),
        grid=(dbsz,),
        in_specs=[
            pl.BlockSpec((1, n_tok, width), lambda b: (b, 0, 0)),
            pl.BlockSpec((1, n_heads, dk, dv), lambda b: (b, 0, 0, 0)),
            pl.BlockSpec(cos.shape, lambda b: (0, 0)),
            pl.BlockSpec(sin.shape, lambda b: (0, 0)),
            pl.BlockSpec((n_heads, dv), lambda b: (0, 0)),
        ],
        out_specs=[
            pl.BlockSpec((1, n_tok, n_heads * dv), lambda b: (b, 0, 0)),
            pl.BlockSpec((1, n_heads, dk, dv), lambda b: (b, 0, 0, 0)),
        ],
        out_shape=[
            jax.ShapeDtypeStruct((dbsz, n_tok, n_heads * dv), BF16),
            jax.ShapeDtypeStruct(state.shape, F32),
        ],
        compiler_params=_params(("arbitrary",), est),
        name="retention_sample",
    )(zr3, state, cos, sin, g_ret)


def _suffix_body(pt_ref, lf_hbm, lfn_ref, suf_ref, cs_ref, buf, sem, *, dbsz, n_pages, n_heads, page):
    b = pl.program_id(0)

    def page_copy(seq_idx, p):
        return pltpu.make_async_copy(lf_hbm.at[pt_ref[seq_idx, p]], buf.at[seq_idx * n_pages + p], sem.at[seq_idx])

    @pl.when(b == 0)
    def _():
        def start(r, carry):
            page_copy(r // n_pages, r % n_pages).start()
            return carry

        lax.fori_loop(0, dbsz * n_pages, start, 0)

    def wait(p, carry):
        page_copy(b, p).wait()
        return carry

    lax.fori_loop(0, n_pages, wait, 0)

    r_i = lax.broadcasted_iota(jnp.int32, (page, 2 * page), 0)
    c_i = lax.broadcasted_iota(jnp.int32, (page, 2 * page), 1)
    sel = jnp.where(c_i < page, (r_i > c_i).astype(F32), 1.0)
    rows = n_pages * n_heads
    x = buf[pl.ds(b * n_pages, n_pages)].reshape(rows, page)
    both = jnp.dot(x, sel, precision=lax.Precision.HIGHEST, preferred_element_type=F32)
    total = both[:, page:]
    later = total
    step = n_heads
    while step < rows:
        later = later + jnp.concatenate([later[step:], jnp.zeros((step, page), F32)], axis=0)
        step *= 2
    suf_ref[...] = ((both[:, :page] + (later - total)) * LOG2E).reshape(n_pages, n_heads, page)

    x_new = lfn_ref[0]
    acc = [x_new[0:1, :]]
    for t in range(1, x_new.shape[0]):
        acc.append(acc[-1] + x_new[t:t + 1, :])
    cs_ref[0] = jnp.concatenate(acc, axis=0)


def _sample_forget_bias(page_table, cache_lf_t, lf_new):
    dbsz, n_pages = page_table.shape
    _, n_heads, page = cache_lf_t.shape
    n_tok = lf_new.shape[1]
    n_rows = dbsz * n_pages
    grid_spec = pltpu.PrefetchScalarGridSpec(
        num_scalar_prefetch=1,
        grid=(dbsz,),
        in_specs=[
            pl.BlockSpec(memory_space=pl.ANY),
            pl.BlockSpec((1, n_tok, n_heads), lambda b, pt: (b, 0, 0)),
        ],
        out_specs=[
            pl.BlockSpec((n_pages, n_heads, page), lambda b, pt: (b, 0, 0)),
            pl.BlockSpec((1, n_tok, n_heads), lambda b, pt: (b, 0, 0)),
        ],
        scratch_shapes=[pltpu.VMEM((n_rows, n_heads, page), F32), pltpu.SemaphoreType.DMA((dbsz,))],
    )
    est = _nbytes((n_rows, n_heads, page), F32) + 12 * _nbytes((n_pages * n_heads, 2 * page), F32)
    return pl.pallas_call(
        functools.partial(_suffix_body, dbsz=dbsz, n_pages=n_pages, n_heads=n_heads, page=page),
        grid_spec=grid_spec,
        out_shape=[
            jax.ShapeDtypeStruct((n_rows, n_heads, page), F32),
            jax.ShapeDtypeStruct((dbsz, n_tok, n_heads), F32),
        ],
        compiler_params=_params(("arbitrary",), est),
        name="sample_forget_bias",
    )(page_table, cache_lf_t, lf_new)


def _up_attn_body(pt_ref, a_ref, b_ref, q_ref, kn_ref, vn_ref, bias_ref, csc_ref, bn_ref, *rest,
                  n_pg, n_groups, n_heads, scale):
    k_refs = rest[:n_pg]
    v_refs = rest[n_pg:2 * n_pg]
    up_ref, o_ref = rest[2 * n_pg:2 * n_pg + 2]
    rowbias_ref, m_ref, l_ref, acc_ref = rest[2 * n_pg + 2:]

    j = (pl.program_id(0) * pl.num_programs(1) + pl.program_id(1)) % n_groups
    rows, cols = rowbias_ref.shape

    @pl.when(j == 0)
    def _():
        rh = lax.broadcasted_iota(jnp.int32, (rows, cols), 0) % n_heads
        ch = lax.broadcasted_iota(jnp.int32, (rows, cols), 1) % n_heads
        rowbias_ref[...] = jnp.where(rh == ch, csc_ref[0][:, :1] * LOG2E, -jnp.inf)
        m_ref[...] = jnp.full_like(m_ref, -jnp.inf)
        l_ref[...] = jnp.zeros_like(l_ref)
        acc_ref[...] = jnp.zeros_like(acc_ref)

    q = q_ref[0] * (scale * LOG2E)
    rowbias = rowbias_ref[...]

    s_pages = []
    for i in range(n_pg):
        s = lax.dot_general(q, k_refs[i][0], NT_DIMS, preferred_element_type=F32)
        s_pages.append(s + (rowbias + bias_ref[0, 0, i:i + 1, :]))
    up = jnp.dot(a_ref[...], b_ref[...], preferred_element_type=F32)
    up_ref[...] = jnp.square(jnp.maximum(up, 0.0)).astype(up_ref.dtype)

    m_old = m_ref[...]
    m_new = m_old
    for s in s_pages:
        m_new = jnp.maximum(m_new, jnp.max(s, axis=1, keepdims=True))
    alpha = jnp.exp2(m_old - m_new)
    l_new = alpha * l_ref[...]
    acc_new = alpha * acc_ref[...]
    for i in range(n_pg):
        p = jnp.exp2(s_pages[i] - m_new)
        l_new = l_new + jnp.sum(p, axis=1, keepdims=True)
        acc_new = acc_new + jnp.dot(p, v_refs[i][0], preferred_element_type=F32)
    m_ref[...] = m_new
    l_ref[...] = l_new
    acc_ref[...] = acc_new

    @pl.when(j == n_groups - 1)
    def _():
        sn = lax.dot_general(q, kn_ref[0], NT_DIMS, preferred_element_type=F32) + bn_ref[0] * LOG2E
        m2 = jnp.maximum(m_new, jnp.max(sn, axis=1, keepdims=True))
        a2 = jnp.exp2(m_new - m2)
        p2 = jnp.exp2(sn - m2)
        l2 = a2 * l_new + jnp.sum(p2, axis=1, keepdims=True)
        o = (a2 * acc_new + jnp.dot(p2, vn_ref[0], preferred_element_type=F32)) / l2
        o_ref[0] = o.astype(o_ref.dtype)


MAX_PAGES_PER_STEP = 4


def _mlp_up_with_sample_attention(h2, w_up, page_table, q3, kn_pad, vn_pad, bias_rows, cs_col, bias_new,
                                  cache_k, cache_v, n_heads, tn):
    m, kdim = h2.shape
    n = w_up.shape[1]
    dbsz, rows, head_dim = q3.shape
    n_pages = page_table.shape[1]
    cols = cache_k.shape[1]
    pad_rows = kn_pad.shape[1]
    total_pages = dbsz * n_pages
    tm = min(m, 1024)
    while total_pages // ((n // tn) * (m // tm)) > MAX_PAGES_PER_STEP:
        tm //= 2
    gj, gi = n // tn, m // tm
    assert m % tm == 0 and tm % 16 == 0 and n % tn == 0 and total_pages % (gj * gi) == 0
    n_pg = total_pages // (gj * gi)
    assert n_pages % n_pg == 0
    n_groups = n_pages // n_pg

    def seq_of(j, i):
        return (j * gi + i) // n_groups

    def grp_of(j, i):
        return (j * gi + i) % n_groups

    def per_seq(block):
        return pl.BlockSpec(block, lambda j, i, pt: (seq_of(j, i), 0, 0))

    def page_spec(idx):
        return pl.BlockSpec((1, cols, head_dim),
                            lambda j, i, pt, idx=idx: (pt[seq_of(j, i), grp_of(j, i) * n_pg + idx], 0, 0))

    grid_spec = pltpu.PrefetchScalarGridSpec(
        num_scalar_prefetch=1,
        grid=(gj, gi),
        in_specs=[
            pl.BlockSpec((tm, kdim), lambda j, i, pt: (i, 0)),
            pl.BlockSpec((kdim, tn), lambda j, i, pt: (0, j)),
            per_seq((1, rows, head_dim)),
            per_seq((1, pad_rows, head_dim)),
            per_seq((1, pad_rows, head_dim)),
            pl.BlockSpec((1, 1, n_pg, cols), lambda j, i, pt: (seq_of(j, i), grp_of(j, i), 0, 0)),
            per_seq((1, rows, LANES)),
            per_seq((1, rows, LANES)),
        ] + [page_spec(idx) for idx in range(n_pg)] * 2,
        out_specs=[
            pl.BlockSpec((tm, tn), lambda j, i, pt: (i, j)),
            per_seq((1, rows, head_dim)),
        ],
        scratch_shapes=[
            pltpu.VMEM((rows, cols), F32),
            pltpu.VMEM((rows, 1), F32),
            pltpu.VMEM((rows, 1), F32),
            pltpu.VMEM((rows, head_dim), F32),
        ],
    )
    est = 2 * (_nbytes((tm, kdim), h2.dtype) + _nbytes((kdim, tn), w_up.dtype) + _nbytes((tm, tn), BF16)) \
        + 2 * _nbytes((tm, tn), F32) + 4 * n_pg * _nbytes((cols, head_dim), F32) \
        + (3 * n_pg + 4) * _nbytes((rows, cols), F32)
    return pl.pallas_call(
        functools.partial(_up_attn_body, n_pg=n_pg, n_groups=n_groups, n_heads=n_heads, scale=head_dim ** -0.5),
        grid_spec=grid_spec,
        out_shape=[
            jax.ShapeDtypeStruct((m, n), BF16),
            jax.ShapeDtypeStruct((dbsz, rows, head_dim), BF16),
        ],
        compiler_params=_params(("arbitrary", "arbitrary"), est),
        name="mlp_up_fox_sample",
    )(page_table, h2, w_up, q3, kn_pad, vn_pad, bias_rows.reshape(dbsz, n_groups, n_pg, cols), cs_col, bias_new,
      *([cache_k] * n_pg), *([cache_v] * n_pg))


def _mix_inputs(x, w, lw, tm):
    d = x.shape[1]
    fox_w = lw["fox_w"]
    h, lf_pad = _norm_fgate(x, w["g_mix"], lw["wf_rows"], lw["bf_pad"], tm)
    tn = 512
    qkv = [
        _mm("in_proj_fox", [h], [lw["w_in_t"]], fox_w, b_offs=[o * fox_w], b_rows=[True], epi=_epi_id,
            out_dtype=F32, tm=tm, tn=tn)
        for o in range(3)
    ]
    zr = _mm("in_proj_ret", [h], [lw["w_in_t"]], lw["ret_w"], b_offs=[lw["ret_off"]], b_rows=[True], epi=_epi_id,
             out_dtype=F32, tm=tm, tn=tn)
    gates = _mm("merge_gates", [h], [w["w_gate"]], 2 * d, extras=[(w["b_gate"].reshape(1, 2 * d), 0)],
                epi=_epi_sigmoid_bias, out_dtype=F32, tm=tm, tn=tn)
    return lf_pad, qkv, zr, gates


def _merge_branches(x, oa, ob, gates, w, tm):
    d = x.shape[1]
    tn = 512
    merged = _mm("branch_merge", [oa, ob], [w["w_pa"], w["w_pb"]], d, extras=[(gates, 0), (gates, d)],
                 epi=_epi_merge, out_dtype=BF16, tm=tm, tn=tn)
    x1 = _mm("out_proj", [merged], [w["w_o"]], d, extras=[(x, 0)], epi=_epi_residual, out_dtype=F32, tm=tm, tn=tn)
    return x1, _rmsnorm(x1, w["g_mlp"], BF16, min(tm, 512))


def _finish(x1, up, p, x1_s, up_s, p_s, w, tm, last):
    d = x1.shape[1]
    ms = x1_s.shape[0]
    tn = 512
    wide = up.shape[1] > 4096 and d % 1024 == 0
    x2, x2_s = _mm("mlp_down", [up], [w["w_down"]], d, extras=[(x1, 0)], epi=_epi_residual, out_dtype=F32,
                   tm=tm, tn=1024 if wide else tn, tk=2048 if wide else None, rider=([up_s], [(x1_s, 0)]))
    h3 = _rmsnorm(x2, w["g_ple"], BF16, min(tm, 512))
    h3_s = _rmsnorm(x2_s, w["g_ple"], BF16, ms)
    x3, x3_s = _mm("ple", [h3, p], [w["w_ple_gate"], w["w_ple"]], d, extras=[(x2, 0)], epi=_epi_ple,
                   out_dtype=F32, tm=tm, tn=tn, rider=([h3_s, p_s], [(x2_s, 0)]))
    if last is not None:
        return _rmsnorm(x3, last, F32, min(tm, 512)), _rmsnorm(x3_s, last, F32, ms)
    return x3, x3_s


def kernel(x_prompt, x_sample, cache_k, cache_v, cache_logf, state_ret, page_table, p_prompt, p_sample, g_mix, w_in,
           b_f, g_ret, w_pa, w_pb, w_gate, b_gate, w_o, g_mlp, w_up, w_down, g_ple, w_ple_gate, w_ple, g_final):
    bsz, seq, d = x_prompt.shape
    dbsz, dseq, _ = x_sample.shape
    depth, n_phys, page, ha, da = cache_k.shape
    _, _, hr, dk, dv = state_ret.shape
    n_pages = page_table.shape[1]
    past = n_pages * page
    fox_w = ha * da
    f_off = 3 * fox_w
    tp, ts = bsz * seq, dbsz * dseq
    rows = dseq * ha

    cos_p, sin_p = _rope_tables(seq, dk // 2, 0)
    cos_s, sin_s = _rope_tables(8, dk // 2, past)

    xp = x_prompt.reshape(tp, d)
    xs = x_sample.reshape(ts, d)
    outs = [[] for _ in range(8)]
    for i in range(depth):
        w = dict(g_mix=g_mix[i], w_pa=w_pa[i], w_pb=w_pb[i], w_gate=w_gate[i], b_gate=b_gate[i],
                 w_o=w_o[i], g_mlp=g_mlp[i], w_up=w_up[i], w_down=w_down[i], g_ple=g_ple[i],
                 w_ple_gate=w_ple_gate[i], w_ple=w_ple[i])
        w_in_t = jnp.swapaxes(w_in[i], 0, 1)
        lw = dict(
            fox_w=fox_w,
            w_in_t=w_in_t,
            wf_rows=jnp.pad(w_in_t[f_off:f_off + ha], ((0, LANES - ha), (0, 0))),
            bf_pad=jnp.pad(b_f[i].reshape(1, ha), ((0, 0), (0, LANES - ha))),
            ret_off=f_off + ha,
            ret_w=w_in_t.shape[0] - f_off - ha,
        )
        last = g_final if i == depth - 1 else None

        lf_pad, (q, k, v), zr, gates = _mix_inputs(xp, w, lw, 1024)
        ct = _prompt_cumsum(lf_pad, bsz, seq, ha)
        oa = _fox_prompt(q, k, v, ct, bsz, seq, ha, da)
        ob, sp = _ret_prompt(zr, cos_p, sin_p, g_ret[i], bsz, seq, hr, dk, dv)
        x1p, h2p = _merge_branches(xp, oa, ob, gates, w, 1024)
        outs[0].append(k.reshape(bsz, seq, ha, da))
        outs[1].append(v.reshape(bsz, seq, ha, da))
        outs[2].append(lf_pad[:, :ha].reshape(bsz, seq, ha))
        outs[3].append(sp)

        lf_pad, (q, k, v), zr, gates = _mix_inputs(xs, w, lw, ts)
        lf_new = lf_pad[:, :ha].reshape(dbsz, dseq, ha)
        suf, cs = _sample_forget_bias(page_table, jnp.swapaxes(cache_logf[i], 1, 2), lf_new)
        bias_rows = jnp.swapaxes(suf.reshape(dbsz, n_pages, ha, page), 2, 3).reshape(dbsz, n_pages, page * ha)
        cs_col = jnp.broadcast_to(cs.reshape(dbsz, rows, 1), (dbsz, rows, LANES))
        diff = cs[:, :, :, None, None] - cs[:, None, None, :, :]
        tt = jnp.arange(dseq)
        hh = jnp.arange(ha)
        keep = (tt[None, None, :, None] <= tt[:, None, None, None]) & (hh[None, :, None, None] == hh[None, None, None, :])
        bias_new = jnp.where(keep[None], diff, -jnp.inf).reshape(dbsz, rows, rows)
        bias_new = jnp.pad(bias_new, ((0, 0), (0, 0), (0, LANES - rows)), constant_values=-jnp.inf)
        pad_new = lambda t: jnp.pad(t.reshape(dbsz, rows, da), ((0, 0), (0, LANES - rows), (0, 0)))
        up_p, oa = _mlp_up_with_sample_attention(
            h2p, w["w_up"], page_table, q.reshape(dbsz, rows, da), pad_new(k), pad_new(v), bias_rows, cs_col,
            bias_new, cache_k[i].reshape(n_phys, page * ha, da), cache_v[i].reshape(n_phys, page * ha, da), ha, 512)

        ob, ss = _ret_sample(zr.reshape(dbsz, dseq, -1), state_ret[i], cos_s, sin_s, g_ret[i])
        x1s, h2s = _merge_branches(xs, oa.reshape(ts, fox_w), ob.reshape(ts, hr * dv), gates, w, ts)
        up_s = _mm("mlp_up", [h2s], [w["w_up"]], w["w_up"].shape[1], epi=_epi_relu2, out_dtype=BF16, tm=ts, tn=512)
        xp, xs = _finish(x1p, up_p, p_prompt[i].reshape(tp, -1), x1s, up_s, p_sample[i].reshape(ts, -1), w, 1024,
                         last)
        outs[4].append(k.reshape(dbsz, dseq, ha, da))
        outs[5].append(v.reshape(dbsz, dseq, ha, da))
        outs[6].append(lf_new)
        outs[7].append(ss)

    return (xp.reshape(bsz, seq, d), xs.reshape(dbsz, dseq, d), *[jnp.stack(o) for o in outs])
```

```python
import functools

import jax
import jax.numpy as jnp
from jax import lax
from jax.experimental import pallas as pl
from jax.experimental.pallas import tpu as pltpu

F32 = jnp.float32
BF16 = jnp.bfloat16

NORM_EPS = 1e-6
GN_EPS = 1e-5
ROPE_BASE = 10000.0
LOG2E = 1.4426950408889634

LANES = 128
VMEM_CAP_BYTES = 60 * 1024 * 1024
VMEM_SLACK_BYTES = 6 * 1024 * 1024

NT_DIMS = (((1,), (1,)), ((), ()))
TN_DIMS = (((0,), (0,)), ((), ()))


def _nbytes(shape, dtype):
    n = 1
    for s in shape:
        n *= s
    return n * jnp.dtype(dtype).itemsize


def _params(semantics, est_bytes):
    limit = min(VMEM_CAP_BYTES, max(32 * 1024 * 1024, est_bytes + VMEM_SLACK_BYTES))
    return pltpu.CompilerParams(dimension_semantics=semantics, vmem_limit_bytes=limit)


def _sigmoid(x):
    return 1.0 / (1.0 + jnp.exp(-x))


def _log_sigmoid(x):
    return -(jnp.maximum(-x, 0.0) + jnp.log1p(jnp.exp(-jnp.abs(x))))


def _rmsnorm_body(x_ref, g_ref, o_ref):
    x = x_ref[...]
    var = jnp.mean(x * x, axis=-1, keepdims=True)
    o_ref[...] = (x * lax.rsqrt(var + NORM_EPS) * g_ref[...]).astype(o_ref.dtype)


def _rmsnorm(x, g, out_dtype, tm):
    m, d = x.shape
    est = 2 * (_nbytes((tm, d), x.dtype) + _nbytes((tm, d), out_dtype)) + _nbytes((tm, d), F32)
    return pl.pallas_call(
        _rmsnorm_body,
        grid=(m // tm,),
        in_specs=[pl.BlockSpec((tm, d), lambda i: (i, 0)), pl.BlockSpec((1, d), lambda i: (0, 0))],
        out_specs=pl.BlockSpec((tm, d), lambda i: (i, 0)),
        out_shape=jax.ShapeDtypeStruct((m, d), out_dtype),
        compiler_params=_params(("arbitrary",), est),
        name="rmsnorm",
    )(x, g.reshape(1, d))


def _norm_fgate_body(x_ref, g_ref, wf_ref, bf_ref, h_ref, lf_ref):
    x = x_ref[...]
    var = jnp.mean(x * x, axis=-1, keepdims=True)
    h = x * lax.rsqrt(var + NORM_EPS) * g_ref[...]
    h_ref[...] = h.astype(h_ref.dtype)
    fa = lax.dot_general(h, wf_ref[...], NT_DIMS, preferred_element_type=F32) + bf_ref[...]
    lf_ref[...] = _log_sigmoid(fa)


def _norm_fgate(x, g, wf_rows, bf_pad, tm):
    m, d = x.shape
    est = 2 * (_nbytes((tm, d), F32) + _nbytes((tm, d), BF16) + _nbytes((LANES, d), F32)) + 2 * _nbytes((tm, d), F32)
    return pl.pallas_call(
        _norm_fgate_body,
        grid=(m // tm,),
        in_specs=[
            pl.BlockSpec((tm, d), lambda i: (i, 0)),
            pl.BlockSpec((1, d), lambda i: (0, 0)),
            pl.BlockSpec((LANES, d), lambda i: (0, 0)),
            pl.BlockSpec((1, LANES), lambda i: (0, 0)),
        ],
        out_specs=[pl.BlockSpec((tm, d), lambda i: (i, 0)), pl.BlockSpec((tm, LANES), lambda i: (i, 0))],
        out_shape=[jax.ShapeDtypeStruct((m, d), BF16), jax.ShapeDtypeStruct((m, LANES), F32)],
        compiler_params=_params(("arbitrary",), est),
        name="norm_fgate",
    )(x, g.reshape(1, d), wf_rows, bf_pad)


def _mm_body(*refs, n_dots, n_extra, n_rider_extra, epi, k_steps, b_rows):
    it = iter(refs)
    a_refs = [next(it) for _ in range(n_dots)]
    b_refs = [next(it) for _ in range(n_dots)]
    ex_refs = [next(it) for _ in range(n_extra)]
    has_rider = n_rider_extra is not None
    ra_refs = [next(it) for _ in range(n_dots)] if has_rider else []
    rex_refs = [next(it) for _ in range(n_rider_extra)] if has_rider else []
    o_ref = next(it)
    ro_ref = next(it) if has_rider else None

    def dot(a, b, rows):
        if rows:
            return lax.dot_general(a, b, NT_DIMS, preferred_element_type=F32)
        return jnp.dot(a, b, preferred_element_type=F32)

    def tile(a_rs, ex_rs, out_ref):
        if k_steps == 1:
            accs = [dot(a[...], b[...], r) for a, b, r in zip(a_rs, b_refs, b_rows)]
            out_ref[...] = epi(accs, [e[...] for e in ex_rs]).astype(out_ref.dtype)
            return
        k = pl.program_id(2)
        part = dot(a_rs[0][...], b_refs[0][...], b_rows[0])

        @pl.when(k == 0)
        def _():
            out_ref[...] = ex_rs[0][...] + part

        @pl.when(k > 0)
        def _():
            out_ref[...] += part

    tile(a_refs, ex_refs, o_ref)
    if has_rider:
        @pl.when(pl.program_id(1) == 0)
        def _():
            tile(ra_refs, rex_refs, ro_ref)


def _mm(name, a_list, b_list, n, *, epi, out_dtype, tm, tn, b_offs=None, b_rows=None, extras=(), tk=None,
        rider=None):
    m = a_list[0].shape[0]
    n_dots = len(a_list)
    b_offs = list(b_offs) if b_offs is not None else [0] * n_dots
    b_rows = tuple(b_rows) if b_rows is not None else (False,) * n_dots
    assert m % tm == 0 and n % tn == 0
    assert all(o % tn == 0 or (rows and o % 8 == 0) for o, rows in zip(b_offs, b_rows))
    k0 = a_list[0].shape[1]
    k_steps = 1 if tk is None else k0 // tk
    if k_steps > 1:
        assert n_dots == 1 and k0 % tk == 0 and not b_rows[0] and epi is _epi_residual and out_dtype == F32

    est = 0
    if k_steps == 1:
        grid = (n // tn, m // tm)
        sem = ("arbitrary", "arbitrary")
        a_specs = [pl.BlockSpec((tm, a.shape[1]), lambda j, i: (i, 0)) for a in a_list]
        b_specs = []
        for b, o, rows in zip(b_list, b_offs, b_rows):
            if rows and o % tn:
                b_specs.append(pl.BlockSpec((pl.Element(tn), pl.Element(b.shape[1])),
                                            lambda j, i, o=o: (pl.multiple_of(o + j * tn, 8), 0)))
            elif rows:
                b_specs.append(pl.BlockSpec((tn, b.shape[1]), lambda j, i, o=o // tn: (j + o, 0)))
            else:
                b_specs.append(pl.BlockSpec((b.shape[0], tn), lambda j, i, o=o // tn: (0, j + o)))
        for a in a_list:
            est += 2 * (_nbytes((tm, a.shape[1]), a.dtype) + _nbytes((a.shape[1], tn), F32))
        scratch = []
    else:
        grid = (n // tn, m // tm, k_steps)
        sem = ("arbitrary", "arbitrary", "arbitrary")
        a_specs = [pl.BlockSpec((tm, tk), lambda j, i, k: (i, k))]
        b_specs = [pl.BlockSpec((tk, tn), lambda j, i, k, o=b_offs[0] // tn: (k, j + o))]
        est += 2 * (_nbytes((tm, tk), a_list[0].dtype) + _nbytes((tk, tn), b_list[0].dtype))
        scratch = []
    ex_specs = []
    for arr, off in extras:
        assert off % tn == 0
        rows = 1 if arr.shape[0] == 1 else tm
        if k_steps == 1:
            imap = (lambda j, i, o=off // tn: (0, j + o)) if rows == 1 else (lambda j, i, o=off // tn: (i, j + o))
        else:
            imap = (lambda j, i, k, o=off // tn: (0, j + o)) if rows == 1 else (lambda j, i, k, o=off // tn: (i, j + o))
        ex_specs.append(pl.BlockSpec((rows, tn), imap))
        est += 2 * _nbytes((rows, tn), arr.dtype)
    out_map = (lambda j, i: (i, j)) if k_steps == 1 else (lambda j, i, k: (i, j))
    est += 2 * _nbytes((tm, tn), out_dtype) + (n_dots + 2) * _nbytes((tm, tn), F32)
    in_specs = a_specs + b_specs + ex_specs
    operands = [*a_list, *b_list, *[arr for arr, _ in extras]]
    out_specs = pl.BlockSpec((tm, tn), out_map)
    out_shape = jax.ShapeDtypeStruct((m, n), out_dtype)
    n_rider_extra = None
    if rider is not None:
        r_a_list, r_extras = rider
        ms = r_a_list[0].shape[0]
        n_rider_extra = len(r_extras)
        for a in r_a_list:
            if k_steps == 1:
                in_specs.append(pl.BlockSpec((ms, a.shape[1]), lambda j, i: (0, 0)))
            else:
                in_specs.append(pl.BlockSpec((ms, tk), lambda j, i, k: (0, k)))
            est += 2 * _nbytes((ms, a.shape[1] if k_steps == 1 else tk), a.dtype)
        for arr, off in r_extras:
            assert off % tn == 0 and arr.shape[0] in (1, ms)
            imap = (lambda j, i, o=off // tn: (0, j + o)) if k_steps == 1 else (lambda j, i, k, o=off // tn: (0, j + o))
            in_specs.append(pl.BlockSpec((arr.shape[0], tn), imap))
        operands += [*r_a_list, *[arr for arr, _ in r_extras]]
        r_out_map = (lambda j, i: (0, j)) if k_steps == 1 else (lambda j, i, k: (0, j))
        out_specs = [out_specs, pl.BlockSpec((ms, tn), r_out_map)]
        out_shape = [out_shape, jax.ShapeDtypeStruct((ms, n), out_dtype)]
        est += 4 * _nbytes((ms, tn), F32)
    body = functools.partial(_mm_body, n_dots=n_dots, n_extra=len(extras), n_rider_extra=n_rider_extra, epi=epi,
                             k_steps=k_steps, b_rows=b_rows)
    return pl.pallas_call(
        body,
        grid=grid,
        in_specs=in_specs,
        out_specs=out_specs,
        out_shape=out_shape,
        scratch_shapes=scratch,
        compiler_params=_params(sem, est),
        name=name,
    )(*operands)


def _epi_id(accs, ex):
    return accs[0]


def _epi_sigmoid_bias(accs, ex):
    return _sigmoid(accs[0] + ex[0])


def _epi_merge(accs, ex):
    return ex[0] * accs[0] + ex[1] * accs[1]


def _epi_residual(accs, ex):
    return ex[0] + accs[0]


def _epi_relu2(accs, ex):
    return jnp.square(jnp.maximum(accs[0], 0.0))


def _epi_ple(accs, ex):
    return ex[0] + _sigmoid(accs[0]) * accs[1]


def _rope_body(cos_ref, sin_ref, *, base_pos):
    n, half = cos_ref.shape
    pos = (lax.broadcasted_iota(jnp.int32, (n, half), 0) + base_pos).astype(F32)
    idx = lax.broadcasted_iota(jnp.int32, (n, half), 1).astype(F32)
    inv = ROPE_BASE ** (-idx / half)
    ang = pos * inv
    cos_ref[...] = jnp.cos(ang)
    sin_ref[...] = jnp.sin(ang)


def _rope_tables(n_rows, half, base_pos):
    return pl.pallas_call(
        functools.partial(_rope_body, base_pos=base_pos),
        out_shape=[jax.ShapeDtypeStruct((n_rows, half), F32)] * 2,
        name="rope_tables",
    )()


def _rotate(x, cos, sin):
    half = x.shape[-1] // 2
    x1 = x[:, :half]
    x2 = x[:, half:]
    return jnp.concatenate([x1 * cos - x2 * sin, x2 * cos + x1 * sin], axis=1)


def _cumsum_body(lf_ref, ct_ref, *, blk):
    s_len = lf_ref.shape[0]
    n_heads = ct_ref.shape[1]
    r = lax.broadcasted_iota(jnp.int32, (blk, blk), 0)
    c = lax.broadcasted_iota(jnp.int32, (blk, blk), 1)
    upper = (r <= c).astype(F32)
    carry = jnp.zeros((LANES, 1), F32)
    for j in range(s_len // blk):
        x = lf_ref[j * blk:(j + 1) * blk, :]
        loc = lax.dot_general(x, upper, TN_DIMS, precision=lax.Precision.HIGHEST,
                              preferred_element_type=F32) + carry
        ct_ref[0, :, j * blk:(j + 1) * blk] = loc[:n_heads, :]
        carry = loc[:, blk - 1:blk]


def _prompt_cumsum(lf_pad, bsz, seq, n_heads):
    blk = 256
    return pl.pallas_call(
        functools.partial(_cumsum_body, blk=blk),
        grid=(bsz,),
        in_specs=[pl.BlockSpec((seq, LANES), lambda b: (b, 0))],
        out_specs=pl.BlockSpec((1, n_heads, seq), lambda b: (b, 0, 0)),
        out_shape=jax.ShapeDtypeStruct((bsz, n_heads, seq), F32),
        compiler_params=_params(("arbitrary",), 4 * _nbytes((seq, LANES), F32)),
        name="forget_cumsum",
    )(lf_pad)


def _fox_prompt_body(q_ref, k_ref, v_ref, c_ref, o_ref, *, tq, scale):
    seq = q_ref.shape[0]
    nsub = tq // LANES
    row = lax.broadcasted_iota(jnp.int32, (tq, tq), 0)
    col = lax.broadcasted_iota(jnp.int32, (tq, tq), 1)
    causal = col <= row
    c2 = c_ref[0, 0] * LOG2E
    for i in range(seq // tq):
        kv = (i + 1) * tq
        q = q_ref[i * tq:(i + 1) * tq, :] * (scale * LOG2E)
        cq = jnp.concatenate(
            [jnp.broadcast_to(c2[i * nsub + a:i * nsub + a + 1, :], (LANES, LANES)).T[:, :1]
             for a in range(nsub)], axis=0)
        ck = jnp.concatenate([c2[r:r + 1, :] for r in range(kv // LANES)], axis=1)
        s = lax.dot_general(q, k_ref[0:kv, :], NT_DIMS, preferred_element_type=F32)
        s = s + cq - ck
        diag = jnp.where(causal, s[:, i * tq:], -jnp.inf)
        s = diag if i == 0 else jnp.concatenate([s[:, :i * tq], diag], axis=1)
        m = jnp.max(s, axis=1, keepdims=True)
        p = jnp.exp2(s - m)
        l = jnp.sum(p, axis=1, keepdims=True)
        o = jnp.dot(p, v_ref[0:kv, :], preferred_element_type=F32)
        o_ref[i * tq:(i + 1) * tq, :] = (o / l).astype(o_ref.dtype)


def _fox_prompt(q, k, v, ct, bsz, seq, n_heads, head_dim):
    tq = 256
    c4 = ct.reshape(bsz, n_heads, seq // LANES, LANES)
    est = 8 * _nbytes((seq, head_dim), F32) + 6 * _nbytes((tq, seq), F32)
    blk = pl.BlockSpec((seq, head_dim), lambda b, h: (b, h))
    return pl.pallas_call(
        functools.partial(_fox_prompt_body, tq=tq, scale=head_dim ** -0.5),
        grid=(bsz, n_heads),
        in_specs=[blk, blk, blk, pl.BlockSpec((1, 1, seq // LANES, LANES), lambda b, h: (b, h, 0, 0))],
        out_specs=blk,
        out_shape=jax.ShapeDtypeStruct((bsz * seq, n_heads * head_dim), BF16),
        compiler_params=_params(("arbitrary", "arbitrary"), est),
        name="fox_prompt",
    )(q, k, v, c4)


def _log_gamma(h, shape):
    hv = jnp.full(shape, h, jnp.int32).astype(F32)
    return jnp.log(1.0 - jnp.exp2(-5.0 - hv))


def _retention_decays(lg, c, n_tok):
    ri = lax.broadcasted_iota(jnp.int32, (c, c), 0).astype(F32)
    ci = lax.broadcasted_iota(jnp.int32, (c, c), 1).astype(F32)
    diff = ri - ci
    dmat = jnp.where(diff >= 0, jnp.exp(lg * jnp.maximum(diff, 0.0)), 0.0)
    idx = lax.broadcasted_iota(jnp.int32, (c, 1), 0).astype(F32)
    q_dec = jnp.exp(lg * (idx + 1.0))
    k_dec = jnp.exp(lg * (n_tok - 1.0 - idx))
    chunk_dec = jnp.exp(lg * float(n_tok))
    return dmat, q_dec, k_dec, chunk_dec


def _retention_chunk(q, k, v, state, decays):
    dmat, q_dec, k_dec, chunk_dec = decays
    inner = lax.dot_general(q, k, NT_DIMS, preferred_element_type=F32) * dmat
    o = jnp.dot(inner, v, preferred_element_type=F32)
    o = o + jnp.dot(q, state, preferred_element_type=F32) * q_dec
    new_state = state * chunk_dec + lax.dot_general(k * k_dec, v, TN_DIMS, preferred_element_type=F32)
    return o, new_state


def _group_norm_gate(o, gate, g_row):
    mu = jnp.mean(o, axis=-1, keepdims=True)
    d = o - mu
    var = jnp.mean(d * d, axis=-1, keepdims=True)
    normed = d * lax.rsqrt(var + GN_EPS) * g_row
    return (gate * _sigmoid(gate)) * normed


def _ret_prompt_body(q_ref, k_ref, v_ref, g_ref, cos_ref, sin_ref, gr_ref, o_ref, s_ref, *, chunk, kscale):
    h = pl.program_id(1)
    seq = q_ref.shape[0]
    decays = _retention_decays(_log_gamma(h, (1, 1)), chunk, chunk)
    state = jnp.zeros(s_ref.shape[2:], F32)
    for c in range(seq // chunk):
        rows = slice(c * chunk, (c + 1) * chunk)
        cos = cos_ref[rows, :]
        sin = sin_ref[rows, :]
        q = _rotate(q_ref[rows, :], cos, sin)
        k = _rotate(k_ref[rows, :], cos, sin) * kscale
        o, state = _retention_chunk(q, k, v_ref[rows, :], state, decays)
        o_ref[rows, :] = _group_norm_gate(o, g_ref[rows, :], gr_ref[0]).astype(o_ref.dtype)
    s_ref[0, 0] = state


def _ret_prompt(zr, cos, sin, g_ret, bsz, seq, n_heads, dk, dv):
    chunk = 256
    assert (2 * n_heads * dk) % dv == 0
    kb = n_heads
    vb = 2 * n_heads * dk // dv
    est = 2 * (2 * _nbytes((seq, dk), F32) + 3 * _nbytes((seq, dv), F32)) + 16 * _nbytes((chunk, dk), F32) \
        + 6 * _nbytes((dk, dv), F32)
    return pl.pallas_call(
        functools.partial(_ret_prompt_body, chunk=chunk, kscale=dk ** -0.5),
        grid=(bsz, n_heads),
        in_specs=[
            pl.BlockSpec((seq, dk), lambda b, h: (b, h)),
            pl.BlockSpec((seq, dk), lambda b, h: (b, kb + h)),
            pl.BlockSpec((seq, dv), lambda b, h: (b, vb + h)),
            pl.BlockSpec((seq, dv), lambda b, h: (b, vb + n_heads + h)),
            pl.BlockSpec((seq, dk // 2), lambda b, h: (0, 0)),
            pl.BlockSpec((seq, dk // 2), lambda b, h: (0, 0)),
            pl.BlockSpec((1, 1, dv), lambda b, h: (h, 0, 0)),
        ],
        out_specs=[
            pl.BlockSpec((seq, dv), lambda b, h: (b, h)),
            pl.BlockSpec((1, 1, dk, dv), lambda b, h: (b, h, 0, 0)),
        ],
        out_shape=[
            jax.ShapeDtypeStruct((bsz * seq, n_heads * dv), BF16),
            jax.ShapeDtypeStruct((bsz, n_heads, dk, dv), F32),
        ],
        compiler_params=_params(("arbitrary", "arbitrary"), est),
        name="retention_prompt",
    )(zr, zr, zr, zr, cos, sin, g_ret.reshape(n_heads, 1, dv))


def _ret_sample_body(z_ref, st_ref, cos_ref, sin_ref, gr_ref, o_ref, so_ref, *, n_heads, dk, dv, n_tok, kscale):
    cos = cos_ref[:n_tok, :]
    sin = sin_ref[:n_tok, :]
    kw = n_heads * dk
    vw = n_heads * dv
    for h in range(n_heads):
        q = _rotate(z_ref[0, :, h * dk:(h + 1) * dk], cos, sin)
        k = _rotate(z_ref[0, :, kw + h * dk:kw + (h + 1) * dk], cos, sin) * kscale
        v = z_ref[0, :, 2 * kw + h * dv:2 * kw + (h + 1) * dv]
        gate = z_ref[0, :, 2 * kw + vw + h * dv:2 * kw + vw + (h + 1) * dv]
        decays = _retention_decays(_log_gamma(h, (1, 1)), n_tok, n_tok)
        o, new_state = _retention_chunk(q, k, v, st_ref[0, h], decays)
        so_ref[0, h] = new_state
        o_ref[0, :, h * dv:(h + 1) * dv] = _group_norm_gate(o, gate, gr_ref[h:h + 1, :]).astype(o_ref.dtype)


def _ret_sample(zr3, state, cos, sin, g_ret):
    dbsz, n_tok, width = zr3.shape
    _, n_heads, dk, dv = state.shape
    est = 4 * _nbytes((n_heads, dk, dv), F32) + 4 * _nbytes((8, width), F32) + 4 * _nbytes((dk, dv), F32)
    return pl.pallas_call(
        functools.partial(_ret_sample_body, n_heads=n_heads, dk=dk, dv=dv, n_tok=n_tok, kscale=dk ** -0.5),
        grid=(dbsz,),
        in_specs=[
            pl.BlockSpec((1, n_tok, width), lambda b: (b, 0, 0)),
            pl.BlockSpec((1, n_heads, dk, dv), lambda b: (b, 0, 0, 0)),
            pl.BlockSpec(cos.shape, lambda b: (0, 0)),
            pl.BlockSpec(sin.shape, lambda b: (0, 0)),
            pl.BlockSpec((n_heads, dv), lambda b: (0, 0)),
        ],
        out_specs=[
            pl.BlockSpec((1, n_tok, n_heads * dv), lambda b: (b, 0, 0)),
            pl.BlockSpec((1, n_heads, dk, dv), lambda b: (b, 0, 0, 0)),
        ],
        out_shape=[
            jax.ShapeDtypeStruct((dbsz, n_tok, n_heads * dv), BF16),
            jax.ShapeDtypeStruct(state.shape, F32),
        ],
        compiler_params=_params(("arbitrary",), est),
        name="retention_sample",
    )(zr3, state, cos, sin, g_ret)


def _suffix_body(pt_ref, lf_hbm, lfn_ref, suf_ref, cs_ref, buf, sem, *, dbsz, n_pages, n_heads, page):
    b = pl.program_id(0)

    def page_copy(seq_idx, p):
        return pltpu.make_async_copy(lf_hbm.at[pt_ref[seq_idx, p]], buf.at[seq_idx * n_pages + p], sem.at[seq_idx])

    @pl.when(b == 0)
    def _():
        def start(r, carry):
            page_copy(r // n_pages, r % n_pages).start()
            return carry

        lax.fori_loop(0, dbsz * n_pages, start, 0)

    def wait(p, carry):
        page_copy(b, p).wait()
        return carry

    lax.fori_loop(0, n_pages, wait, 0)

    r_i = lax.broadcasted_iota(jnp.int32, (page, 2 * page), 0)
    c_i = lax.broadcasted_iota(jnp.int32, (page, 2 * page), 1)
    sel = jnp.where(c_i < page, (r_i > c_i).astype(F32), 1.0)
    rows = n_pages * n_heads
    x = buf[pl.ds(b * n_pages, n_pages)].reshape(rows, page)
    both = jnp.dot(x, sel, precision=lax.Precision.HIGHEST, preferred_element_type=F32)
    total = both[:, page:]
    later = total
    step = n_heads
    while step < rows:
        later = later + jnp.concatenate([later[step:], jnp.zeros((step, page), F32)], axis=0)
        step *= 2
    suf_ref[...] = ((both[:, :page] + (later - total)) * LOG2E).reshape(n_pages, n_heads, page)

    x_new = lfn_ref[0]
    acc = [x_new[0:1, :]]
    for t in range(1, x_new.shape[0]):
        acc.append(acc[-1] + x_new[t:t + 1, :])
    cs_ref[0] = jnp.concatenate(acc, axis=0)


def _sample_forget_bias(page_table, cache_lf_t, lf_new):
    dbsz, n_pages = page_table.shape
    _, n_heads, page = cache_lf_t.shape
    n_tok = lf_new.shape[1]
    n_rows = dbsz * n_pages
    grid_spec = pltpu.PrefetchScalarGridSpec(
        num_scalar_prefetch=1,
        grid=(dbsz,),
        in_specs=[
            pl.BlockSpec(memory_space=pl.ANY),
            pl.BlockSpec((1, n_tok, n_heads), lambda b, pt: (b, 0, 0)),
        ],
        out_specs=[
            pl.BlockSpec((n_pages, n_heads, page), lambda b, pt: (b, 0, 0)),
            pl.BlockSpec((1, n_tok, n_heads), lambda b, pt: (b, 0, 0)),
        ],
        scratch_shapes=[pltpu.VMEM((n_rows, n_heads, page), F32), pltpu.SemaphoreType.DMA((dbsz,))],
    )
    est = _nbytes((n_rows, n_heads, page), F32) + 12 * _nbytes((n_pages * n_heads, 2 * page), F32)
    return pl.pallas_call(
        functools.partial(_suffix_body, dbsz=dbsz, n_pages=n_pages, n_heads=n_heads, page=page),
        grid_spec=grid_spec,
        out_shape=[
            jax.ShapeDtypeStruct((n_rows, n_heads, page), F32),
            jax.ShapeDtypeStruct((dbsz, n_tok, n_heads), F32),
        ],
        compiler_params=_params(("arbitrary",), est),
        name="sample_forget_bias",
    )(page_table, cache_lf_t, lf_new)


def _up_attn_body(pt_ref, a_ref, b_ref, q_ref, kn_ref, vn_ref, bias_ref, csc_ref, bn_ref, *rest,
                  n_pg, n_groups, n_heads, scale):
    k_refs = rest[:n_pg]
    v_refs = rest[n_pg:2 * n_pg]
    up_ref, o_ref = rest[2 * n_pg:2 * n_pg + 2]
    rowbias_ref, m_ref, l_ref, acc_ref = rest[2 * n_pg + 2:]

    j = (pl.program_id(0) * pl.num_programs(1) + pl.program_id(1)) % n_groups
    rows, cols = rowbias_ref.shape

    @pl.when(j == 0)
    def _():
        rh = lax.broadcasted_iota(jnp.int32, (rows, cols), 0) % n_heads
        ch = lax.broadcasted_iota(jnp.int32, (rows, cols), 1) % n_heads
        rowbias_ref[...] = jnp.where(rh == ch, csc_ref[0][:, :1] * LOG2E, -jnp.inf)
        m_ref[...] = jnp.full_like(m_ref, -jnp.inf)
        l_ref[...] = jnp.zeros_like(l_ref)
        acc_ref[...] = jnp.zeros_like(acc_ref)

    q = q_ref[0] * (scale * LOG2E)
    rowbias = rowbias_ref[...]

    s_pages = []
    for i in range(n_pg):
        s = lax.dot_general(q, k_refs[i][0], NT_DIMS, preferred_element_type=F32)
        s_pages.append(s + (rowbias + bias_ref[0, 0, i:i + 1, :]))
    up = jnp.dot(a_ref[...], b_ref[...], preferred_element_type=F32)
    up_ref[...] = jnp.square(jnp.maximum(up, 0.0)).astype(up_ref.dtype)

    m_old = m_ref[...]
    m_new = m_old
    for s in s_pages:
        m_new = jnp.maximum(m_new, jnp.max(s, axis=1, keepdims=True))
    alpha = jnp.exp2(m_old - m_new)
    l_new = alpha * l_ref[...]
    acc_new = alpha * acc_ref[...]
    for i in range(n_pg):
        p = jnp.exp2(s_pages[i] - m_new)
        l_new = l_new + jnp.sum(p, axis=1, keepdims=True)
        acc_new = acc_new + jnp.dot(p, v_refs[i][0], preferred_element_type=F32)
    m_ref[...] = m_new
    l_ref[...] = l_new
    acc_ref[...] = acc_new

    @pl.when(j == n_groups - 1)
    def _():
        sn = lax.dot_general(q, kn_ref[0], NT_DIMS, preferred_element_type=F32) + bn_ref[0] * LOG2E
        m2 = jnp.maximum(m_new, jnp.max(sn, axis=1, keepdims=True))
        a2 = jnp.exp2(m_new - m2)
        p2 = jnp.exp2(sn - m2)
        l2 = a2 * l_new + jnp.sum(p2, axis=1, keepdims=True)
        o = (a2 * acc_new + jnp.dot(p2, vn_ref[0], preferred_element_type=F32)) / l2
        o_ref[0] = o.astype(o_ref.dtype)


MAX_PAGES_PER_STEP = 4


def _mlp_up_with_sample_attention(h2, w_up, page_table, q3, kn_pad, vn_pad, bias_rows, cs_col, bias_new,
                                  cache_k, cache_v, n_heads, tn):
    m, kdim = h2.shape
    n = w_up.shape[1]
    dbsz, rows, head_dim = q3.shape
    n_pages = page_table.shape[1]
    cols = cache_k.shape[1]
    pad_rows = kn_pad.shape[1]
    total_pages = dbsz * n_pages
    tm = min(m, 1024)
    while total_pages // ((n // tn) * (m // tm)) > MAX_PAGES_PER_STEP:
        tm //= 2
    gj, gi = n // tn, m // tm
    assert m % tm == 0 and tm % 16 == 0 and n % tn == 0 and total_pages % (gj * gi) == 0
    n_pg = total_pages // (gj * gi)
    assert n_pages % n_pg == 0
    n_groups = n_pages // n_pg

    def seq_of(j, i):
        return (j * gi + i) // n_groups

    def grp_of(j, i):
        return (j * gi + i) % n_groups

    def per_seq(block):
        return pl.BlockSpec(block, lambda j, i, pt: (seq_of(j, i), 0, 0))

    def page_spec(idx):
        return pl.BlockSpec((1, cols, head_dim),
                            lambda j, i, pt, idx=idx: (pt[seq_of(j, i), grp_of(j, i) * n_pg + idx], 0, 0))

    grid_spec = pltpu.PrefetchScalarGridSpec(
        num_scalar_prefetch=1,
        grid=(gj, gi),
        in_specs=[
            pl.BlockSpec((tm, kdim), lambda j, i, pt: (i, 0)),
            pl.BlockSpec((kdim, tn), lambda j, i, pt: (0, j)),
            per_seq((1, rows, head_dim)),
            per_seq((1, pad_rows, head_dim)),
            per_seq((1, pad_rows, head_dim)),
            pl.BlockSpec((1, 1, n_pg, cols), lambda j, i, pt: (seq_of(j, i), grp_of(j, i), 0, 0)),
            per_seq((1, rows, LANES)),
            per_seq((1, rows, LANES)),
        ] + [page_spec(idx) for idx in range(n_pg)] * 2,
        out_specs=[
            pl.BlockSpec((tm, tn), lambda j, i, pt: (i, j)),
            per_seq((1, rows, head_dim)),
        ],
        scratch_shapes=[
            pltpu.VMEM((rows, cols), F32),
            pltpu.VMEM((rows, 1), F32),
            pltpu.VMEM((rows, 1), F32),
            pltpu.VMEM((rows, head_dim), F32),
        ],
    )
    est = 2 * (_nbytes((tm, kdim), h2.dtype) + _nbytes((kdim, tn), w_up.dtype) + _nbytes((tm, tn), BF16)) \
        + 2 * _nbytes((tm, tn), F32) + 4 * n_pg * _nbytes((cols, head_dim), F32) \
        + (3 * n_pg + 4) * _nbytes((rows, cols), F32)
    return pl.pallas_call(
        functools.partial(_up_attn_body, n_pg=n_pg, n_groups=n_groups, n_heads=n_heads, scale=head_dim ** -0.5),
        grid_spec=grid_spec,
        out_shape=[
            jax.ShapeDtypeStruct((m, n), BF16),
            jax.ShapeDtypeStruct((dbsz, rows, head_dim), BF16),
        ],
        compiler_params=_params(("arbitrary", "arbitrary"), est),
        name="mlp_up_fox_sample",
    )(page_table, h2, w_up, q3, kn_pad, vn_pad, bias_rows.reshape(dbsz, n_groups, n_pg, cols), cs_col, bias_new,
      *([cache_k] * n_pg), *([cache_v] * n_pg))


def _mix_inputs(x, x_s, w, lw, tm):
    d = x.shape[1]
    fox_w = lw["fox_w"]
    h, lf_pad = _norm_fgate(x, w["g_mix"], lw["wf_rows"], lw["bf_pad"], tm)
    h_s, lf_pad_s = _norm_fgate(x_s, w["g_mix"], lw["wf_rows"], lw["bf_pad"], x_s.shape[0])
    tn = 512
    qkv = [
        _mm("in_proj_fox", [h], [lw["w_in_t"]], fox_w, b_offs=[o * fox_w], b_rows=[True], epi=_epi_id,
            out_dtype=F32, tm=tm, tn=tn, rider=([h_s], []))
        for o in range(3)
    ]
    zr = _mm("in_proj_ret", [h], [lw["w_in_t"]], lw["ret_w"], b_offs=[lw["ret_off"]], b_rows=[True], epi=_epi_id,
             out_dtype=F32, tm=tm, tn=tn, rider=([h_s], []))
    b_gate = w["b_gate"].reshape(1, 2 * d)
    gates = _mm("merge_gates", [h], [w["w_gate"]], 2 * d, extras=[(b_gate, 0)], epi=_epi_sigmoid_bias,
                out_dtype=F32, tm=tm, tn=tn, rider=([h_s], [(b_gate, 0)]))
    return (lf_pad, lf_pad_s), qkv, zr, gates


def _merge_branches(x, oa, ob, gates, w, tm):
    d = x.shape[1]
    tn = 512
    merged = _mm("branch_merge", [oa, ob], [w["w_pa"], w["w_pb"]], d, extras=[(gates, 0), (gates, d)],
                 epi=_epi_merge, out_dtype=BF16, tm=tm, tn=tn)
    x1 = _mm("out_proj", [merged], [w["w_o"]], d, extras=[(x, 0)], epi=_epi_residual, out_dtype=F32, tm=tm, tn=tn)
    return x1, _rmsnorm(x1, w["g_mlp"], BF16, min(tm, 512))


def _finish(x1, up, p, x1_s, up_s, p_s, w, tm, last):
    d = x1.shape[1]
    ms = x1_s.shape[0]
    tn = 512
    wide = up.shape[1] > 4096 and d % 1024 == 0
    x2, x2_s = _mm("mlp_down", [up], [w["w_down"]], d, extras=[(x1, 0)], epi=_epi_residual, out_dtype=F32,
                   tm=tm, tn=1024 if wide else tn, tk=2048 if wide else None, rider=([up_s], [(x1_s, 0)]))
    h3 = _rmsnorm(x2, w["g_ple"], BF16, min(tm, 512))
    h3_s = _rmsnorm(x2_s, w["g_ple"], BF16, ms)
    x3, x3_s = _mm("ple", [h3, p], [w["w_ple_gate"], w["w_ple"]], d, extras=[(x2, 0)], epi=_epi_ple,
                   out_dtype=F32, tm=tm, tn=tn, rider=([h3_s, p_s], [(x2_s, 0)]))
    if last is not None:
        return _rmsnorm(x3, last, F32, min(tm, 512)), _rmsnorm(x3_s, last, F32, ms)
    return x3, x3_s


def kernel(x_prompt, x_sample, cache_k, cache_v, cache_logf, state_ret, page_table, p_prompt, p_sample, g_mix, w_in,
           b_f, g_ret, w_pa, w_pb, w_gate, b_gate, w_o, g_mlp, w_up, w_down, g_ple, w_ple_gate, w_ple, g_final):
    bsz, seq, d = x_prompt.shape
    dbsz, dseq, _ = x_sample.shape
    depth, n_phys, page, ha, da = cache_k.shape
    _, _, hr, dk, dv = state_ret.shape
    n_pages = page_table.shape[1]
    past = n_pages * page
    fox_w = ha * da
    f_off = 3 * fox_w
    tp, ts = bsz * seq, dbsz * dseq
    rows = dseq * ha

    cos_p, sin_p = _rope_tables(seq, dk // 2, 0)
    cos_s, sin_s = _rope_tables(8, dk // 2, past)

    xp = x_prompt.reshape(tp, d)
    xs = x_sample.reshape(ts, d)
    outs = [[] for _ in range(8)]
    for i in range(depth):
        w = dict(g_mix=g_mix[i], w_pa=w_pa[i], w_pb=w_pb[i], w_gate=w_gate[i], b_gate=b_gate[i],
                 w_o=w_o[i], g_mlp=g_mlp[i], w_up=w_up[i], w_down=w_down[i], g_ple=g_ple[i],
                 w_ple_gate=w_ple_gate[i], w_ple=w_ple[i])
        w_in_t = jnp.swapaxes(w_in[i], 0, 1)
        lw = dict(
            fox_w=fox_w,
            w_in_t=w_in_t,
            wf_rows=jnp.pad(w_in_t[f_off:f_off + ha], ((0, LANES - ha), (0, 0))),
            bf_pad=jnp.pad(b_f[i].reshape(1, ha), ((0, 0), (0, LANES - ha))),
            ret_off=f_off + ha,
            ret_w=w_in_t.shape[0] - f_off - ha,
        )
        last = g_final if i == depth - 1 else None

        (lf_pad, lf_pad_s), qkv, (zr, zr_s), (gates, gates_s) = _mix_inputs(xp, xs, w, lw, 1024)
        (q, q_s), (k, k_s), (v, v_s) = qkv

        ct = _prompt_cumsum(lf_pad, bsz, seq, ha)
        oa = _fox_prompt(q, k, v, ct, bsz, seq, ha, da)
        ob, sp = _ret_prompt(zr, cos_p, sin_p, g_ret[i], bsz, seq, hr, dk, dv)
        x1p, h2p = _merge_branches(xp, oa, ob, gates, w, 1024)
        outs[0].append(k.reshape(bsz, seq, ha, da))
        outs[1].append(v.reshape(bsz, seq, ha, da))
        outs[2].append(lf_pad[:, :ha].reshape(bsz, seq, ha))
        outs[3].append(sp)

        q, k, v, zr, gates = q_s, k_s, v_s, zr_s, gates_s
        lf_new = lf_pad_s[:, :ha].reshape(dbsz, dseq, ha)
        suf, cs = _sample_forget_bias(page_table, jnp.swapaxes(cache_logf[i], 1, 2), lf_new)
        bias_rows = jnp.swapaxes(suf.reshape(dbsz, n_pages, ha, page), 2, 3).reshape(dbsz, n_pages, page * ha)
        cs_col = jnp.broadcast_to(cs.reshape(dbsz, rows, 1), (dbsz, rows, LANES))
        diff = cs[:, :, :, None, None] - cs[:, None, None, :, :]
        tt = jnp.arange(dseq)
        hh = jnp.arange(ha)
        keep = (tt[None, None, :, None] <= tt[:, None, None, None]) & (hh[None, :, None, None] == hh[None, None, None, :])
        bias_new = jnp.where(keep[None], diff, -jnp.inf).reshape(dbsz, rows, rows)
        bias_new = jnp.pad(bias_new, ((0, 0), (0, 0), (0, LANES - rows)), constant_values=-jnp.inf)
        pad_new = lambda t: jnp.pad(t.reshape(dbsz, rows, da), ((0, 0), (0, LANES - rows), (0, 0)))
        up_p, oa = _mlp_up_with_sample_attention(
            h2p, w["w_up"], page_table, q.reshape(dbsz, rows, da), pad_new(k), pad_new(v), bias_rows, cs_col,
            bias_new, cache_k[i].reshape(n_phys, page * ha, da), cache_v[i].reshape(n_phys, page * ha, da), ha, 512)

        ob, ss = _ret_sample(zr.reshape(dbsz, dseq, -1), state_ret[i], cos_s, sin_s, g_ret[i])
        x1s, h2s = _merge_branches(xs, oa.reshape(ts, fox_w), ob.reshape(ts, hr * dv), gates, w, ts)
        up_s = _mm("mlp_up", [h2s], [w["w_up"]], w["w_up"].shape[1], epi=_epi_relu2, out_dtype=BF16, tm=ts, tn=512)
        xp, xs = _finish(x1p, up_p, p_prompt[i].reshape(tp, -1), x1s, up_s, p_sample[i].reshape(ts, -1), w, 1024,
                         last)
        outs[4].append(k.reshape(dbsz, dseq, ha, da))
        outs[5].append(v.reshape(dbsz, dseq, ha, da))
        outs[6].append(lf_new)
        outs[7].append(ss)

    return (xp.reshape(bsz, seq, d), xs.reshape(dbsz, dseq, d), *[jnp.stack(o) for o in outs])
```

```python
import functools

import jax
import jax.numpy as jnp
from jax import lax
from jax.experimental import pallas as pl
from jax.experimental.pallas import tpu as pltpu

F32 = jnp.float32
BF16 = jnp.bfloat16

NORM_EPS = 1e-6
GN_EPS = 1e-5
ROPE_BASE = 10000.0
LOG2E = 1.4426950408889634

LANES = 128
VMEM_CAP_BYTES = 60 * 1024 * 1024
VMEM_SLACK_BYTES = 6 * 1024 * 1024

NT_DIMS = (((1,), (1,)), ((), ()))
TN_DIMS = (((0,), (0,)), ((), ()))


def _nbytes(shape, dtype):
    n = 1
    for s in shape:
        n *= s
    return n * jnp.dtype(dtype).itemsize


def _params(semantics, est_bytes):
    limit = min(VMEM_CAP_BYTES, max(32 * 1024 * 1024, est_bytes + VMEM_SLACK_BYTES))
    return pltpu.CompilerParams(dimension_semantics=semantics, vmem_limit_bytes=limit)


def _sigmoid(x):
    return 1.0 / (1.0 + jnp.exp(-x))


def _log_sigmoid(x):
    return -(jnp.maximum(-x, 0.0) + jnp.log1p(jnp.exp(-jnp.abs(x))))


def _rmsnorm_body(x_ref, g_ref, o_ref):
    x = x_ref[...]
    var = jnp.mean(x * x, axis=-1, keepdims=True)
    o_ref[...] = (x * lax.rsqrt(var + NORM_EPS) * g_ref[...]).astype(o_ref.dtype)


def _rmsnorm(x, g, out_dtype, tm):
    m, d = x.shape
    est = 2 * (_nbytes((tm, d), x.dtype) + _nbytes((tm, d), out_dtype)) + _nbytes((tm, d), F32)
    return pl.pallas_call(
        _rmsnorm_body,
        grid=(m // tm,),
        in_specs=[pl.BlockSpec((tm, d), lambda i: (i, 0)), pl.BlockSpec((1, d), lambda i: (0, 0))],
        out_specs=pl.BlockSpec((tm, d), lambda i: (i, 0)),
        out_shape=jax.ShapeDtypeStruct((m, d), out_dtype),
        compiler_params=_params(("arbitrary",), est),
        name="rmsnorm",
    )(x, g.reshape(1, d))


def _norm_fgate_body(x_ref, g_ref, wf_ref, bf_ref, h_ref, lf_ref):
    x = x_ref[...]
    var = jnp.mean(x * x, axis=-1, keepdims=True)
    h = x * lax.rsqrt(var + NORM_EPS) * g_ref[...]
    h_ref[...] = h.astype(h_ref.dtype)
    fa = lax.dot_general(h, wf_ref[...], NT_DIMS, preferred_element_type=F32) + bf_ref[...]
    lf_ref[...] = _log_sigmoid(fa)


def _norm_fgate(x, g, wf_rows, bf_pad, tm):
    m, d = x.shape
    est = 2 * (_nbytes((tm, d), F32) + _nbytes((tm, d), BF16) + _nbytes((LANES, d), F32)) + 2 * _nbytes((tm, d), F32)
    return pl.pallas_call(
        _norm_fgate_body,
        grid=(m // tm,),
        in_specs=[
            pl.BlockSpec((tm, d), lambda i: (i, 0)),
            pl.BlockSpec((1, d), lambda i: (0, 0)),
            pl.BlockSpec((LANES, d), lambda i: (0, 0)),
            pl.BlockSpec((1, LANES), lambda i: (0, 0)),
        ],
        out_specs=[pl.BlockSpec((tm, d), lambda i: (i, 0)), pl.BlockSpec((tm, LANES), lambda i: (i, 0))],
        out_shape=[jax.ShapeDtypeStruct((m, d), BF16), jax.ShapeDtypeStruct((m, LANES), F32)],
        compiler_params=_params(("arbitrary",), est),
        name="norm_fgate",
    )(x, g.reshape(1, d), wf_rows, bf_pad)


def _mm_body(*refs, n_dots, n_extra, n_rider_extra, epi, k_steps, b_rows):
    it = iter(refs)
    a_refs = [next(it) for _ in range(n_dots)]
    b_refs = [next(it) for _ in range(n_dots)]
    ex_refs = [next(it) for _ in range(n_extra)]
    has_rider = n_rider_extra is not None
    ra_refs = [next(it) for _ in range(n_dots)] if has_rider else []
    rex_refs = [next(it) for _ in range(n_rider_extra)] if has_rider else []
    o_ref = next(it)
    ro_ref = next(it) if has_rider else None

    def dot(a, b, rows):
        if rows:
            return lax.dot_general(a, b, NT_DIMS, preferred_element_type=F32)
        return jnp.dot(a, b, preferred_element_type=F32)

    def tile(a_rs, ex_rs, out_ref):
        if k_steps == 1:
            accs = [dot(a[...], b[...], r) for a, b, r in zip(a_rs, b_refs, b_rows)]
            out_ref[...] = epi(accs, [e[...] for e in ex_rs]).astype(out_ref.dtype)
            return
        k = pl.program_id(2)
        part = dot(a_rs[0][...], b_refs[0][...], b_rows[0])

        @pl.when(k == 0)
        def _():
            out_ref[...] = ex_rs[0][...] + part

        @pl.when(k > 0)
        def _():
            out_ref[...] += part

    tile(a_refs, ex_refs, o_ref)
    if has_rider:
        @pl.when(pl.program_id(1) == 0)
        def _():
            tile(ra_refs, rex_refs, ro_ref)


def _mm(name, a_list, b_list, n, *, epi, out_dtype, tm, tn, b_offs=None, b_rows=None, extras=(), tk=None,
        rider=None):
    m = a_list[0].shape[0]
    n_dots = len(a_list)
    b_offs = list(b_offs) if b_offs is not None else [0] * n_dots
    b_rows = tuple(b_rows) if b_rows is not None else (False,) * n_dots
    assert m % tm == 0 and n % tn == 0
    assert all(o % tn == 0 or (rows and o % 8 == 0) for o, rows in zip(b_offs, b_rows))
    k0 = a_list[0].shape[1]
    k_steps = 1 if tk is None else k0 // tk
    if k_steps > 1:
        assert n_dots == 1 and k0 % tk == 0 and not b_rows[0] and epi is _epi_residual and out_dtype == F32

    est = 0
    if k_steps == 1:
        grid = (n // tn, m // tm)
        sem = ("arbitrary", "arbitrary")
        a_specs = [pl.BlockSpec((tm, a.shape[1]), lambda j, i: (i, 0)) for a in a_list]
        b_specs = []
        for b, o, rows in zip(b_list, b_offs, b_rows):
            if rows and o % tn:
                b_specs.append(pl.BlockSpec((pl.Element(tn), pl.Element(b.shape[1])),
                                            lambda j, i, o=o: (pl.multiple_of(o + j * tn, 8), 0)))
            elif rows:
                b_specs.append(pl.BlockSpec((tn, b.shape[1]), lambda j, i, o=o // tn: (j + o, 0)))
            else:
                b_specs.append(pl.BlockSpec((b.shape[0], tn), lambda j, i, o=o // tn: (0, j + o)))
        for a in a_list:
            est += 2 * (_nbytes((tm, a.shape[1]), a.dtype) + _nbytes((a.shape[1], tn), F32))
        scratch = []
    else:
        grid = (n // tn, m // tm, k_steps)
        sem = ("arbitrary", "arbitrary", "arbitrary")
        a_specs = [pl.BlockSpec((tm, tk), lambda j, i, k: (i, k))]
        b_specs = [pl.BlockSpec((tk, tn), lambda j, i, k, o=b_offs[0] // tn: (k, j + o))]
        est += 2 * (_nbytes((tm, tk), a_list[0].dtype) + _nbytes((tk, tn), b_list[0].dtype))
        scratch = []
    ex_specs = []
    for arr, off in extras:
        assert off % tn == 0
        rows = 1 if arr.shape[0] == 1 else tm
        if k_steps == 1:
            imap = (lambda j, i, o=off // tn: (0, j + o)) if rows == 1 else (lambda j, i, o=off // tn: (i, j + o))
        else:
            imap = (lambda j, i, k, o=off // tn: (0, j + o)) if rows == 1 else (lambda j, i, k, o=off // tn: (i, j + o))
        ex_specs.append(pl.BlockSpec((rows, tn), imap))
        est += 2 * _nbytes((rows, tn), arr.dtype)
    out_map = (lambda j, i: (i, j)) if k_steps == 1 else (lambda j, i, k: (i, j))
    est += 2 * _nbytes((tm, tn), out_dtype) + (n_dots + 2) * _nbytes((tm, tn), F32)
    in_specs = a_specs + b_specs + ex_specs
    operands = [*a_list, *b_list, *[arr for arr, _ in extras]]
    out_specs = pl.BlockSpec((tm, tn), out_map)
    out_shape = jax.ShapeDtypeStruct((m, n), out_dtype)
    n_rider_extra = None
    if rider is not None:
        r_a_list, r_extras = rider
        ms = r_a_list[0].shape[0]
        n_rider_extra = len(r_extras)
        for a in r_a_list:
            if k_steps == 1:
                in_specs.append(pl.BlockSpec((ms, a.shape[1]), lambda j, i: (0, 0)))
            else:
                in_specs.append(pl.BlockSpec((ms, tk), lambda j, i, k: (0, k)))
            est += 2 * _nbytes((ms, a.shape[1] if k_steps == 1 else tk), a.dtype)
        for arr, off in r_extras:
            assert off % tn == 0 and arr.shape[0] in (1, ms)
            imap = (lambda j, i, o=off // tn: (0, j + o)) if k_steps == 1 else (lambda j, i, k, o=off // tn: (0, j + o))
            in_specs.append(pl.BlockSpec((arr.shape[0], tn), imap))
        operands += [*r_a_list, *[arr for arr, _ in r_extras]]
        r_out_map = (lambda j, i: (0, j)) if k_steps == 1 else (lambda j, i, k: (0, j))
        out_specs = [out_specs, pl.BlockSpec((ms, tn), r_out_map)]
        out_shape = [out_shape, jax.ShapeDtypeStruct((ms, n), out_dtype)]
        est += 4 * _nbytes((ms, tn), F32)
    body = functools.partial(_mm_body, n_dots=n_dots, n_extra=len(extras), n_rider_extra=n_rider_extra, epi=epi,
                             k_steps=k_steps, b_rows=b_rows)
    return pl.pallas_call(
        body,
        grid=grid,
        in_specs=in_specs,
        out_specs=out_specs,
        out_shape=out_shape,
        scratch_shapes=scratch,
        compiler_params=_params(sem, est),
        name=name,
    )(*operands)


def _epi_id(accs, ex):
    return accs[0]


def _epi_sigmoid_bias(accs, ex):
    return _sigmoid(accs[0] + ex[0])


def _epi_merge(accs, ex):
    return ex[0] * accs[0] + ex[1] * accs[1]


def _epi_residual(accs, ex):
    return ex[0] + accs[0]


def _epi_relu2(accs, ex):
    return jnp.square(jnp.maximum(accs[0], 0.0))


def _epi_ple(accs, ex):
    return ex[0] + _sigmoid(accs[0]) * accs[1]


def _rope_body(cos_ref, sin_ref, *, base_pos):
    n, half = cos_ref.shape
    pos = (lax.broadcasted_iota(jnp.int32, (n, half), 0) + base_pos).astype(F32)
    idx = lax.broadcasted_iota(jnp.int32, (n, half), 1).astype(F32)
    inv = ROPE_BASE ** (-idx / half)
    ang = pos * inv
    cos_ref[...] = jnp.cos(ang)
    sin_ref[...] = jnp.sin(ang)


def _rope_tables(n_rows, half, base_pos):
    return pl.pallas_call(
        functools.partial(_rope_body, base_pos=base_pos),
        out_shape=[jax.ShapeDtypeStruct((n_rows, half), F32)] * 2,
        name="rope_tables",
    )()


def _rotate(x, cos, sin):
    half = x.shape[-1] // 2
    x1 = x[:, :half]
    x2 = x[:, half:]
    return jnp.concatenate([x1 * cos - x2 * sin, x2 * cos + x1 * sin], axis=1)


def _cumsum_body(lf_ref, ct_ref, *, blk):
    s_len = lf_ref.shape[0]
    n_heads = ct_ref.shape[1]
    r = lax.broadcasted_iota(jnp.int32, (blk, blk), 0)
    c = lax.broadcasted_iota(jnp.int32, (blk, blk), 1)
    upper = (r <= c).astype(F32)
    carry = jnp.zeros((LANES, 1), F32)
    for j in range(s_len // blk):
        x = lf_ref[j * blk:(j + 1) * blk, :]
        loc = lax.dot_general(x, upper, TN_DIMS, precision=lax.Precision.HIGHEST,
                              preferred_element_type=F32) + carry
        ct_ref[0, :, j * blk:(j + 1) * blk] = loc[:n_heads, :]
        carry = loc[:, blk - 1:blk]


def _prompt_cumsum(lf_pad, bsz, seq, n_heads):
    blk = 256
    return pl.pallas_call(
        functools.partial(_cumsum_body, blk=blk),
        grid=(bsz,),
        in_specs=[pl.BlockSpec((seq, LANES), lambda b: (b, 0))],
        out_specs=pl.BlockSpec((1, n_heads, seq), lambda b: (b, 0, 0)),
        out_shape=jax.ShapeDtypeStruct((bsz, n_heads, seq), F32),
        compiler_params=_params(("arbitrary",), 4 * _nbytes((seq, LANES), F32)),
        name="forget_cumsum",
    )(lf_pad)


def _fox_prompt_body(q_ref, k_ref, v_ref, c_ref, o_ref, *, tq, scale):
    seq = q_ref.shape[0]
    nsub = tq // LANES
    row = lax.broadcasted_iota(jnp.int32, (tq, tq), 0)
    col = lax.broadcasted_iota(jnp.int32, (tq, tq), 1)
    causal = col <= row
    c2 = c_ref[0, 0] * LOG2E
    for i in range(seq // tq):
        kv = (i + 1) * tq
        q = q_ref[i * tq:(i + 1) * tq, :] * (scale * LOG2E)
        cq = jnp.concatenate(
            [jnp.broadcast_to(c2[i * nsub + a:i * nsub + a + 1, :], (LANES, LANES)).T[:, :1]
             for a in range(nsub)], axis=0)
        ck = jnp.concatenate([c2[r:r + 1, :] for r in range(kv // LANES)], axis=1)
        s = lax.dot_general(q, k_ref[0:kv, :], NT_DIMS, preferred_element_type=F32)
        s = s + cq - ck
        diag = jnp.where(causal, s[:, i * tq:], -jnp.inf)
        s = diag if i == 0 else jnp.concatenate([s[:, :i * tq], diag], axis=1)
        m = jnp.max(s, axis=1, keepdims=True)
        p = jnp.exp2(s - m)
        l = jnp.sum(p, axis=1, keepdims=True)
        o = jnp.dot(p, v_ref[0:kv, :], preferred_element_type=F32)
        o_ref[i * tq:(i + 1) * tq, :] = (o / l).astype(o_ref.dtype)


def _fox_prompt(q, k, v, ct, bsz, seq, n_heads, head_dim):
    tq = 256
    c4 = ct.reshape(bsz, n_heads, seq // LANES, LANES)
    est = 8 * _nbytes((seq, head_dim), F32) + 6 * _nbytes((tq, seq), F32)
    blk = pl.BlockSpec((seq, head_dim), lambda b, h: (b, h))
    return pl.pallas_call(
        functools.partial(_fox_prompt_body, tq=tq, scale=head_dim ** -0.5),
        grid=(bsz, n_heads),
        in_specs=[blk, blk, blk, pl.BlockSpec((1, 1, seq // LANES, LANES), lambda b, h: (b, h, 0, 0))],
        out_specs=blk,
        out_shape=jax.ShapeDtypeStruct((bsz * seq, n_heads * head_dim), BF16),
        compiler_params=_params(("arbitrary", "arbitrary"), est),
        name="fox_prompt",
    )(q, k, v, c4)


def _log_gamma(h, shape):
    hv = jnp.full(shape, h, jnp.int32).astype(F32)
    return jnp.log(1.0 - jnp.exp2(-5.0 - hv))


def _retention_decays(lg, c, n_tok):
    ri = lax.broadcasted_iota(jnp.int32, (c, c), 0).astype(F32)
    ci = lax.broadcasted_iota(jnp.int32, (c, c), 1).astype(F32)
    diff = ri - ci
    dmat = jnp.where(diff >= 0, jnp.exp(lg * jnp.maximum(diff, 0.0)), 0.0)
    idx = lax.broadcasted_iota(jnp.int32, (c, 1), 0).astype(F32)
    q_dec = jnp.exp(lg * (idx + 1.0))
    k_dec = jnp.exp(lg * (n_tok - 1.0 - idx))
    chunk_dec = jnp.exp(lg * float(n_tok))
    return dmat, q_dec, k_dec, chunk_dec


def _retention_chunk(q, k, v, state, decays):
    dmat, q_dec, k_dec, chunk_dec = decays
    inner = lax.dot_general(q, k, NT_DIMS, preferred_element_type=F32) * dmat
    o = jnp.dot(inner, v, preferred_element_type=F32)
    o = o + jnp.dot(q, state, preferred_element_type=F32) * q_dec
    new_state = state * chunk_dec + lax.dot_general(k * k_dec, v, TN_DIMS, preferred_element_type=F32)
    return o, new_state


def _group_norm_gate(o, gate, g_row):
    mu = jnp.mean(o, axis=-1, keepdims=True)
    d = o - mu
    var = jnp.mean(d * d, axis=-1, keepdims=True)
    normed = d * lax.rsqrt(var + GN_EPS) * g_row
    return (gate * _sigmoid(gate)) * normed


def _ret_prompt_body(q_ref, k_ref, v_ref, g_ref, cos_ref, sin_ref, gr_ref, o_ref, s_ref, *, chunk, kscale):
    h = pl.program_id(1)
    seq = q_ref.shape[0]
    decays = _retention_decays(_log_gamma(h, (1, 1)), chunk, chunk)
    state = jnp.zeros(s_ref.shape[2:], F32)
    for c in range(seq // chunk):
        rows = slice(c * chunk, (c + 1) * chunk)
        cos = cos_ref[rows, :]
        sin = sin_ref[rows, :]
        q = _rotate(q_ref[rows, :], cos, sin)
        k = _rotate(k_ref[rows, :], cos, sin) * kscale
        o, state = _retention_chunk(q, k, v_ref[rows, :], state, decays)
        o_ref[rows, :] = _group_norm_gate(o, g_ref[rows, :], gr_ref[0]).astype(o_ref.dtype)
    s_ref[0, 0] = state


def _ret_prompt(zr, cos, sin, g_ret, bsz, seq, n_heads, dk, dv):
    chunk = 256
    assert (2 * n_heads * dk) % dv == 0
    kb = n_heads
    vb = 2 * n_heads * dk // dv
    est = 2 * (2 * _nbytes((seq, dk), F32) + 3 * _nbytes((seq, dv), F32)) + 16 * _nbytes((chunk, dk), F32) \
        + 6 * _nbytes((dk, dv), F32)
    return pl.pallas_call(
        functools.partial(_ret_prompt_body, chunk=chunk, kscale=dk ** -0.5),
        grid=(bsz, n_heads),
        in_specs=[
            pl.BlockSpec((seq, dk), lambda b, h: (b, h)),
            pl.BlockSpec((seq, dk), lambda b, h: (b, kb + h)),
            pl.BlockSpec((seq, dv), lambda b, h: (b, vb + h)),
            pl.BlockSpec((seq, dv), lambda b, h: (b, vb + n_heads + h)),
            pl.BlockSpec((seq, dk // 2), lambda b, h: (0, 0)),
            pl.BlockSpec((seq, dk // 2), lambda b, h: (0, 0)),
            pl.BlockSpec((1, 1, dv), lambda b, h: (h, 0, 0)),
        ],
        out_specs=[
            pl.BlockSpec((seq, dv), lambda b, h: (b, h)),
            pl.BlockSpec((1, 1, dk, dv), lambda b, h: (b, h, 0, 0)),
        ],
        out_shape=[
            jax.ShapeDtypeStruct((bsz * seq, n_heads * dv), BF16),
            jax.ShapeDtypeStruct((bsz, n_heads, dk, dv), F32),
        ],
        compiler_params=_params(("arbitrary", "arbitrary"), est),
        name="retention_prompt",
    )(zr, zr, zr, zr, cos, sin, g_ret.reshape(n_heads, 1, dv))


def _ret_sample_body(z_ref, st_ref, cos_ref, sin_ref, gr_ref, o_ref, so_ref, *, n_heads, dk, dv, n_tok, kscale):
    cos = cos_ref[:n_tok, :]
    sin = sin_ref[:n_tok, :]
    kw = n_heads * dk
    vw = n_heads * dv
    for h in range(n_heads):
        q = _rotate(z_ref[0, :, h * dk:(h + 1) * dk], cos, sin)
        k = _rotate(z_ref[0, :, kw + h * dk:kw + (h + 1) * dk], cos, sin) * kscale
        v = z_ref[0, :, 2 * kw + h * dv:2 * kw + (h + 1) * dv]
        gate = z_ref[0, :, 2 * kw + vw + h * dv:2 * kw + vw + (h + 1) * dv]
        decays = _retention_decays(_log_gamma(h, (1, 1)), n_tok, n_tok)
        o, new_state = _retention_chunk(q, k, v, st_ref[0, h], decays)
        so_ref[0, h] = new_state
        o_ref[0, :, h * dv:(h + 1) * dv] = _group_norm_gate(o, gate, gr_ref[h:h + 1, :]).astype(o_ref.dtype)


def _ret_sample(zr3, state, cos, sin, g_ret):
    dbsz, n_tok, width = zr3.shape
    _, n_heads, dk, dv = state.shape
    est = 4 * _nbytes((n_heads, dk, dv), F32) + 4 * _nbytes((8, width), F32) + 4 * _nbytes((dk, dv), F32)
    return pl.pallas_call(
        functools.partial(_ret_sample_body, n_heads=n_heads, dk=dk, dv=dv, n_tok=n_tok, kscale=dk ** -0.5),
        grid=(dbsz,),
        in_specs=[
            pl.BlockSpec((1, n_tok, width), lambda b: (b, 0, 0)),
            pl.BlockSpec((1, n_heads, dk, dv), lambda b: (b, 0, 0, 0)),
            pl.BlockSpec(cos.shape, lambda b: (0, 0)),
            pl.BlockSpec(sin.shape, lambda b: (0, 0)),
            pl.BlockSpec((n_heads, dv), lambda b: (0, 0)),
        ],
        out_specs=[
            pl.BlockSpec((1, n_tok, n_heads * dv), lambda b: (b, 0, 0)),
            pl.BlockSpec((1, n_heads, dk, dv), lambda b: (b, 0, 0, 0)),
        ],
        out_shape=[
            jax.ShapeDtypeStruct((dbsz, n_tok, n_heads * dv), BF16),
            jax.ShapeDtypeStruct(state.shape, F32),
        ],
        compiler_params=_params(("arbitrary",), est),
        name="retention_sample",
    )(zr3, state, cos, sin, g_ret)


def _suffix_body(pt_ref, lf_hbm, lfn_ref, suf_ref, cs_ref, buf, sem, *, dbsz, n_pages, n_heads, page):
    b = pl.program_id(0)

    def page_copy(seq_idx, p):
        return pltpu.make_async_copy(lf_hbm.at[pt_ref[seq_idx, p]], buf.at[seq_idx * n_pages + p], sem.at[seq_idx])

    @pl.when(b == 0)
    def _():
        def start(r, carry):
            page_copy(r // n_pages, r % n_pages).start()
            return carry

        lax.fori_loop(0, dbsz * n_pages, start, 0)

    def wait(p, carry):
        page_copy(b, p).wait()
        return carry

    lax.fori_loop(0, n_pages, wait, 0)

    r_i = lax.broadcasted_iota(jnp.int32, (page, 2 * page), 0)
    c_i = lax.broadcasted_iota(jnp.int32, (page, 2 * page), 1)
    sel = jnp.where(c_i < page, (r_i > c_i).astype(F32), 1.0)
    rows = n_pages * n_heads
    x = buf[pl.ds(b * n_pages, n_pages)].reshape(rows, page)
    both = jnp.dot(x, sel, precision=lax.Precision.HIGHEST, preferred_element_type=F32)
    total = both[:, page:]
    later = total
    step = n_heads
    while step < rows:
        later = later + jnp.concatenate([later[step:], jnp.zeros((step, page), F32)], axis=0)
        step *= 2
    suf_ref[...] = ((both[:, :page] + (later - total)) * LOG2E).reshape(n_pages, n_heads, page)

    x_new = lfn_ref[0]
    acc = [x_new[0:1, :]]
    for t in range(1, x_new.shape[0]):
        acc.append(acc[-1] + x_new[t:t + 1, :])
    cs_ref[0] = jnp.concatenate(acc, axis=0)


def _sample_forget_bias(page_table, cache_lf_t, lf_new):
    dbsz, n_pages = page_table.shape
    _, n_heads, page = cache_lf_t.shape
    n_tok = lf_new.shape[1]
    n_rows = dbsz * n_pages
    grid_spec = pltpu.PrefetchScalarGridSpec(
        num_scalar_prefetch=1,
        grid=(dbsz,),
        in_specs=[
            pl.BlockSpec(memory_space=pl.ANY),
            pl.BlockSpec((1, n_tok, n_heads), lambda b, pt: (b, 0, 0)),
        ],
        out_specs=[
            pl.BlockSpec((n_pages, n_heads, page), lambda b, pt: (b, 0, 0)),
            pl.BlockSpec((1, n_tok, n_heads), lambda b, pt: (b, 0, 0)),
        ],
        scratch_shapes=[pltpu.VMEM((n_rows, n_heads, page), F32), pltpu.SemaphoreType.DMA((dbsz,))],
    )
    est = _nbytes((n_rows, n_heads, page), F32) + 12 * _nbytes((n_pages * n_heads, 2 * page), F32)
    return pl.pallas_call(
        functools.partial(_suffix_body, dbsz=dbsz, n_pages=n_pages, n_heads=n_heads, page=page),
        grid_spec=grid_spec,
        out_shape=[
            jax.ShapeDtypeStruct((n_rows, n_heads, page), F32),
            jax.ShapeDtypeStruct((dbsz, n_tok, n_heads), F32),
        ],
        compiler_params=_params(("arbitrary",), est),
        name="sample_forget_bias",
    )(page_table, cache_lf_t, lf_new)


def _up_attn_body(pt_ref, a_ref, b_ref, q_ref, kn_ref, vn_ref, bias_ref, csc_ref, bn_ref, *rest,
                  n_pg, n_groups, n_heads, scale):
    k_refs = rest[:n_pg]
    v_refs = rest[n_pg:2 * n_pg]
    up_ref, o_ref = rest[2 * n_pg:2 * n_pg + 2]
    rowbias_ref, m_ref, l_ref, acc_ref = rest[2 * n_pg + 2:]

    j = (pl.program_id(0) * pl.num_programs(1) + pl.program_id(1)) % n_groups
    rows, cols = rowbias_ref.shape

    @pl.when(j == 0)
    def _():
        rh = lax.broadcasted_iota(jnp.int32, (rows, cols), 0) % n_heads
        ch = lax.broadcasted_iota(jnp.int32, (rows, cols), 1) % n_heads
        rowbias_ref[...] = jnp.where(rh == ch, csc_ref[0][:, :1] * LOG2E, -jnp.inf)
        m_ref[...] = jnp.full_like(m_ref, -jnp.inf)
        l_ref[...] = jnp.zeros_like(l_ref)
        acc_ref[...] = jnp.zeros_like(acc_ref)

    q = q_ref[0] * (scale * LOG2E)
    rowbias = rowbias_ref[...]

    s_pages = []
    for i in range(n_pg):
        s = lax.dot_general(q, k_refs[i][0], NT_DIMS, preferred_element_type=F32)
        s_pages.append(s + (rowbias + bias_ref[0, 0, i:i + 1, :]))
    up = jnp.dot(a_ref[...], b_ref[...], preferred_element_type=F32)
    up_ref[...] = jnp.square(jnp.maximum(up, 0.0)).astype(up_ref.dtype)

    m_old = m_ref[...]
    m_new = m_old
    for s in s_pages:
        m_new = jnp.maximum(m_new, jnp.max(s, axis=1, keepdims=True))
    alpha = jnp.exp2(m_old - m_new)
    l_new = alpha * l_ref[...]
    acc_new = alpha * acc_ref[...]
    for i in range(n_pg):
        p = jnp.exp2(s_pages[i] - m_new)
        l_new = l_new + jnp.sum(p, axis=1, keepdims=True)
        acc_new = acc_new + jnp.dot(p, v_refs[i][0], preferred_element_type=F32)
    m_ref[...] = m_new
    l_ref[...] = l_new
    acc_ref[...] = acc_new

    @pl.when(j == n_groups - 1)
    def _():
        sn = lax.dot_general(q, kn_ref[0], NT_DIMS, preferred_element_type=F32) + bn_ref[0] * LOG2E
        m2 = jnp.maximum(m_new, jnp.max(sn, axis=1, keepdims=True))
        a2 = jnp.exp2(m_new - m2)
        p2 = jnp.exp2(sn - m2)
        l2 = a2 * l_new + jnp.sum(p2, axis=1, keepdims=True)
        o = (a2 * acc_new + jnp.dot(p2, vn_ref[0], preferred_element_type=F32)) / l2
        o_ref[0] = o.astype(o_ref.dtype)


MAX_PAGES_PER_STEP = 4


def _mlp_up_with_sample_attention(h2, w_up, page_table, q3, kn_pad, vn_pad, bias_rows, cs_col, bias_new,
                                  cache_k, cache_v, n_heads, tn):
    m, kdim = h2.shape
    n = w_up.shape[1]
    dbsz, rows, head_dim = q3.shape
    n_pages = page_table.shape[1]
    cols = cache_k.shape[1]
    pad_rows = kn_pad.shape[1]
    total_pages = dbsz * n_pages
    tm = min(m, 1024)
    while total_pages // ((n // tn) * (m // tm)) > MAX_PAGES_PER_STEP:
        tm //= 2
    gj, gi = n // tn, m // tm
    assert m % tm == 0 and tm % 16 == 0 and n % tn == 0 and total_pages % (gj * gi) == 0
    n_pg = total_pages // (gj * gi)
    assert n_pages % n_pg == 0
    n_groups = n_pages // n_pg

    def seq_of(j, i):
        return (j * gi + i) // n_groups

    def grp_of(j, i):
        return (j * gi + i) % n_groups

    def per_seq(block):
        return pl.BlockSpec(block, lambda j, i, pt: (seq_of(j, i), 0, 0))

    def page_spec(idx):
        return pl.BlockSpec((1, cols, head_dim),
                            lambda j, i, pt, idx=idx: (pt[seq_of(j, i), grp_of(j, i) * n_pg + idx], 0, 0))

    grid_spec = pltpu.PrefetchScalarGridSpec(
        num_scalar_prefetch=1,
        grid=(gj, gi),
        in_specs=[
            pl.BlockSpec((tm, kdim), lambda j, i, pt: (i, 0)),
            pl.BlockSpec((kdim, tn), lambda j, i, pt: (0, j), pipeline_mode=pl.Buffered(1)),
            per_seq((1, rows, head_dim)),
            per_seq((1, pad_rows, head_dim)),
            per_seq((1, pad_rows, head_dim)),
            pl.BlockSpec((1, 1, n_pg, cols), lambda j, i, pt: (seq_of(j, i), grp_of(j, i), 0, 0)),
            per_seq((1, rows, LANES)),
            per_seq((1, rows, LANES)),
        ] + [page_spec(idx) for idx in range(n_pg)] * 2,
        out_specs=[
            pl.BlockSpec((tm, tn), lambda j, i, pt: (i, j)),
            per_seq((1, rows, head_dim)),
        ],
        scratch_shapes=[
            pltpu.VMEM((rows, cols), F32),
            pltpu.VMEM((rows, 1), F32),
            pltpu.VMEM((rows, 1), F32),
            pltpu.VMEM((rows, head_dim), F32),
        ],
    )
    est = 2 * (_nbytes((tm, kdim), h2.dtype) + _nbytes((kdim, tn), w_up.dtype) + _nbytes((tm, tn), BF16)) \
        + 2 * _nbytes((tm, tn), F32) + 4 * n_pg * _nbytes((cols, head_dim), F32) \
        + (3 * n_pg + 4) * _nbytes((rows, cols), F32)
    return pl.pallas_call(
        functools.partial(_up_attn_body, n_pg=n_pg, n_groups=n_groups, n_heads=n_heads, scale=head_dim ** -0.5),
        grid_spec=grid_spec,
        out_shape=[
            jax.ShapeDtypeStruct((m, n), BF16),
            jax.ShapeDtypeStruct((dbsz, rows, head_dim), BF16),
        ],
        compiler_params=_params(("arbitrary", "arbitrary"), est),
        name="mlp_up_fox_sample",
    )(page_table, h2, w_up, q3, kn_pad, vn_pad, bias_rows.reshape(dbsz, n_groups, n_pg, cols), cs_col, bias_new,
      *([cache_k] * n_pg), *([cache_v] * n_pg))


def _mix_inputs(x, x_s, w, lw, tm):
    d = x.shape[1]
    fox_w = lw["fox_w"]
    h, lf_pad = _norm_fgate(x, w["g_mix"], lw["wf_rows"], lw["bf_pad"], tm)
    h_s, lf_pad_s = _norm_fgate(x_s, w["g_mix"], lw["wf_rows"], lw["bf_pad"], x_s.shape[0])
    tn = 512
    qkv = [
        _mm("in_proj_fox", [h], [lw["w_in_t"]], fox_w, b_offs=[o * fox_w], b_rows=[True], epi=_epi_id,
            out_dtype=F32, tm=tm, tn=tn, rider=([h_s], []))
        for o in range(3)
    ]
    zr = _mm("in_proj_ret", [h], [lw["w_in_t"]], lw["ret_w"], b_offs=[lw["ret_off"]], b_rows=[True], epi=_epi_id,
             out_dtype=F32, tm=tm, tn=tn, rider=([h_s], []))
    b_gate = w["b_gate"].reshape(1, 2 * d)
    gates = _mm("merge_gates", [h], [w["w_gate"]], 2 * d, extras=[(b_gate, 0)], epi=_epi_sigmoid_bias,
                out_dtype=F32, tm=tm, tn=tn, rider=([h_s], [(b_gate, 0)]))
    return (lf_pad, lf_pad_s), qkv, zr, gates


def _merge_branches(x, oa, ob, gates, w, tm):
    d = x.shape[1]
    tn = 512
    merged = _mm("branch_merge", [oa, ob], [w["w_pa"], w["w_pb"]], d, extras=[(gates, 0), (gates, d)],
                 epi=_epi_merge, out_dtype=BF16, tm=tm, tn=tn)
    x1 = _mm("out_proj", [merged], [w["w_o"]], d, extras=[(x, 0)], epi=_epi_residual, out_dtype=F32, tm=tm, tn=tn)
    return x1, _rmsnorm(x1, w["g_mlp"], BF16, min(tm, 512))


def _finish(x1, up, p, x1_s, up_s, p_s, w, tm, last):
    d = x1.shape[1]
    ms = x1_s.shape[0]
    tn = 512
    wide = up.shape[1] > 4096 and d % 1024 == 0
    x2, x2_s = _mm("mlp_down", [up], [w["w_down"]], d, extras=[(x1, 0)], epi=_epi_residual, out_dtype=F32,
                   tm=tm, tn=1024 if wide else tn, tk=2048 if wide else None, rider=([up_s], [(x1_s, 0)]))
    h3 = _rmsnorm(x2, w["g_ple"], BF16, min(tm, 512))
    h3_s = _rmsnorm(x2_s, w["g_ple"], BF16, ms)
    x3, x3_s = _mm("ple", [h3, p], [w["w_ple_gate"], w["w_ple"]], d, extras=[(x2, 0)], epi=_epi_ple,
                   out_dtype=F32, tm=tm, tn=tn, rider=([h3_s, p_s], [(x2_s, 0)]))
    if last is not None:
        return _rmsnorm(x3, last, F32, min(tm, 512)), _rmsnorm(x3_s, last, F32, ms)
    return x3, x3_s


def kernel(x_prompt, x_sample, cache_k, cache_v, cache_logf, state_ret, page_table, p_prompt, p_sample, g_mix, w_in,
           b_f, g_ret, w_pa, w_pb, w_gate, b_gate, w_o, g_mlp, w_up, w_down, g_ple, w_ple_gate, w_ple, g_final):
    bsz, seq, d = x_prompt.shape
    dbsz, dseq, _ = x_sample.shape
    depth, n_phys, page, ha, da = cache_k.shape
    _, _, hr, dk, dv = state_ret.shape
    n_pages = page_table.shape[1]
    past = n_pages * page
    fox_w = ha * da
    f_off = 3 * fox_w
    tp, ts = bsz * seq, dbsz * dseq
    rows = dseq * ha

    cos_p, sin_p = _rope_tables(seq, dk // 2, 0)
    cos_s, sin_s = _rope_tables(8, dk // 2, past)

    xp = x_prompt.reshape(tp, d)
    xs = x_sample.reshape(ts, d)
    outs = [[] for _ in range(8)]
    for i in range(depth):
        w = dict(g_mix=g_mix[i], w_pa=w_pa[i], w_pb=w_pb[i], w_gate=w_gate[i], b_gate=b_gate[i],
                 w_o=w_o[i], g_mlp=g_mlp[i], w_up=w_up[i], w_down=w_down[i], g_ple=g_ple[i],
                 w_ple_gate=w_ple_gate[i], w_ple=w_ple[i])
        w_in_t = jnp.swapaxes(w_in[i], 0, 1)
        lw = dict(
            fox_w=fox_w,
            w_in_t=w_in_t,
            wf_rows=jnp.pad(w_in_t[f_off:f_off + ha], ((0, LANES - ha), (0, 0))),
            bf_pad=jnp.pad(b_f[i].reshape(1, ha), ((0, 0), (0, LANES - ha))),
            ret_off=f_off + ha,
            ret_w=w_in_t.shape[0] - f_off - ha,
        )
        last = g_final if i == depth - 1 else None

        (lf_pad, lf_pad_s), qkv, (zr, zr_s), (gates, gates_s) = _mix_inputs(xp, xs, w, lw, 1024)
        (q, q_s), (k, k_s), (v, v_s) = qkv

        ct = _prompt_cumsum(lf_pad, bsz, seq, ha)
        oa = _fox_prompt(q, k, v, ct, bsz, seq, ha, da)
        ob, sp = _ret_prompt(zr, cos_p, sin_p, g_ret[i], bsz, seq, hr, dk, dv)
        x1p, h2p = _merge_branches(xp, oa, ob, gates, w, 1024)
        outs[0].append(k.reshape(bsz, seq, ha, da))
        outs[1].append(v.reshape(bsz, seq, ha, da))
        outs[2].append(lf_pad[:, :ha].reshape(bsz, seq, ha))
        outs[3].append(sp)

        q, k, v, zr, gates = q_s, k_s, v_s, zr_s, gates_s
        lf_new = lf_pad_s[:, :ha].reshape(dbsz, dseq, ha)
        suf, cs = _sample_forget_bias(page_table, jnp.swapaxes(cache_logf[i], 1, 2), lf_new)
        bias_rows = jnp.swapaxes(suf.reshape(dbsz, n_pages, ha, page), 2, 3).reshape(dbsz, n_pages, page * ha)
        cs_col = jnp.broadcast_to(cs.reshape(dbsz, rows, 1), (dbsz, rows, LANES))
        diff = cs[:, :, :, None, None] - cs[:, None, None, :, :]
        tt = jnp.arange(dseq)
        hh = jnp.arange(ha)
        keep = (tt[None, None, :, None] <= tt[:, None, None, None]) & (hh[None, :, None, None] == hh[None, None, None, :])
        bias_new = jnp.where(keep[None], diff, -jnp.inf).reshape(dbsz, rows, rows)
        bias_new = jnp.pad(bias_new, ((0, 0), (0, 0), (0, LANES - rows)), constant_values=-jnp.inf)
        pad_new = lambda t: jnp.pad(t.reshape(dbsz, rows, da), ((0, 0), (0, LANES - rows), (0, 0)))
        up_p, oa = _mlp_up_with_sample_attention(
            h2p, w["w_up"], page_table, q.reshape(dbsz, rows, da), pad_new(k), pad_new(v), bias_rows, cs_col,
            bias_new, cache_k[i].reshape(n_phys, page * ha, da), cache_v[i].reshape(n_phys, page * ha, da), ha, 1024)

        ob, ss = _ret_sample(zr.reshape(dbsz, dseq, -1), state_ret[i], cos_s, sin_s, g_ret[i])
        x1s, h2s = _merge_branches(xs, oa.reshape(ts, fox_w), ob.reshape(ts, hr * dv), gates, w, ts)
        up_s = _mm("mlp_up", [h2s], [w["w_up"]], w["w_up"].shape[1], epi=_epi_relu2, out_dtype=BF16, tm=ts, tn=512)
        xp, xs = _finish(x1p, up_p, p_prompt[i].reshape(tp, -1), x1s, up_s, p_sample[i].reshape(ts, -1), w, 1024,
                         last)
        outs[4].append(k.reshape(dbsz, dseq, ha, da))
        outs[5].append(v.reshape(dbsz, dseq, ha, da))
        outs[6].append(lf_new)
        outs[7].append(ss)

    return (xp.reshape(bsz, seq, d), xs.reshape(dbsz, dseq, d), *[jnp.stack(o) for o in outs])
```
